```python
import math
import jax, jax.numpy as jnp
from jax import lax
import numpy as np

D_MODEL = 2048
BATCH = 4
SEQ = 2048
DEPTH = 2

CHUNK = 64
Q_BLOCK = 128
NORM_EPS = 1e-6
DN_HEADS = 8
DN_HEAD_DIM = 128
DN_WIDTH = DN_HEADS * DN_HEAD_DIM
DN_CONV = 4
SG_GROUPS = 8
SG_GROUP_DIM = 128
SG_WIDTH = SG_GROUPS * SG_GROUP_DIM
SG_BLOCK = 128
MLA_HEADS = 8
MLA_Q_RANK = 512
MLA_KV_RANK = 512
MLA_NOPE = 128
MLA_ROPE = 64
MLA_V = 128
ROPE_THETA = 10000.0
N_BRANCH = 3
BRANCH_WIDTH = 1024
D_FF = 5632
IN_COLS = (4 * DN_WIDTH + 2 * DN_HEADS) + 2 * SG_WIDTH + (MLA_Q_RANK + MLA_KV_RANK + MLA_ROPE) + N_BRANCH * D_MODEL

kernel_name = "hybrid_deltanet_gmlp_mla_macaron"


def rms_norm(x, g, eps=NORM_EPS):
    xf = x.astype(jnp.float32)
    y = xf * lax.rsqrt(jnp.mean(xf * xf, axis=-1, keepdims=True) + eps)
    return (y * g.astype(jnp.float32)).astype(x.dtype)


def layer_norm(x, g, b, eps=1e-5):
    xf = x.astype(jnp.float32)
    mu = jnp.mean(xf, axis=-1, keepdims=True)
    var = jnp.mean(jnp.square(xf - mu), axis=-1, keepdims=True)
    y = (xf - mu) * lax.rsqrt(var + eps) * g.astype(jnp.float32) + b.astype(jnp.float32)
    return y.astype(x.dtype)


def l2_normalize(x, eps=1e-6):
    return x * lax.rsqrt(jnp.sum(x * x, axis=-1, keepdims=True) + eps)


def swiglu(h, w_gate, w_up, w_down):
    return (jax.nn.silu(h @ w_gate) * (h @ w_up)) @ w_down


def causal_conv(x, w):
    k_len, ch = w.shape
    return lax.conv_general_dilated(x, w[:, None, :], window_strides=(1,), padding=[(k_len - 1, 0)],
                                    dimension_numbers=("NWC", "WIO", "NWC"), feature_group_count=ch)


def apply_rope(x, cos, sin):
    xf = x.astype(jnp.float32)
    x1, x2 = jnp.split(xf, 2, axis=-1)
    return jnp.concatenate([x1 * cos - x2 * sin, x1 * sin + x2 * cos], axis=-1).astype(x.dtype)


def _offsets(sizes):
    out, acc = [], 0
    for s in sizes:
        acc += s
        out.append(acc)
    return out


def chunk_gated_delta(q, k, v, g, beta):
    B, S, H, DK = q.shape
    DV = v.shape[-1]
    N = S // CHUNK

    def to_chunks(t):
        return t.reshape((B, N, CHUNK) + t.shape[2:]).swapaxes(2, 3)

    q, k, v, g, beta = (to_chunks(t) for t in (q, k, v, g, beta))
    g = jnp.cumsum(g, axis=-1)
    idx = jnp.arange(CHUNK)
    incl = idx[:, None] >= idx[None, :]
    strict = idx[:, None] > idx[None, :]
    decay = jnp.exp(jnp.where(incl, g[..., :, None] - g[..., None, :], -jnp.inf))
    k_beta = k * beta[..., None]
    a_mat = jnp.where(strict, jnp.einsum("bnhid,bnhjd->bnhij", k_beta, k) * decay, 0.0)
    rhs = jnp.concatenate([k_beta * jnp.exp(g)[..., None], v * beta[..., None]], axis=-1)
    sol = lax.linalg.triangular_solve(a_mat + jnp.eye(CHUNK, dtype=a_mat.dtype), rhs,
                                      left_side=True, lower=True, unit_diagonal=True)
    w, u = sol[..., :DK], sol[..., DK:]
    qk = jnp.einsum("bnhid,bnhjd->bnhij", q, k) * decay

    def step(state, inp):
        q_c, k_c, u_c, w_c, g_c, qk_c = inp
        v_new = u_c - jnp.einsum("bhcd,bhde->bhce", w_c, state)
        o_c = (jnp.einsum("bhcd,bhde->bhce", q_c * jnp.exp(g_c)[..., None], state)
               + jnp.einsum("bhij,bhje->bhie", qk_c, v_new))
        g_last = g_c[..., -1:]
        state = (state * jnp.exp(g_last)[..., None]
                 + jnp.einsum("bhcd,bhce->bhde", k_c * jnp.exp(g_last - g_c)[..., None], v_new))
        return state, o_c

    state0 = jnp.zeros((B, H, DK, DV), q.dtype)
    xs = tuple(t.swapaxes(0, 1) for t in (q, k, u, w, g, qk))
    _, o = lax.scan(step, state0, xs)
    return o.transpose(1, 0, 3, 2, 4).reshape(B, S, H, DV)


def deltanet_branch(qkv, z, a, b, conv_w, a_log, dt_bias, norm_g):
    B, S, _ = qkv.shape
    f32 = jnp.float32
    qkv = jax.nn.silu(causal_conv(qkv, conv_w))
    q, k, v = [t.reshape(B, S, DN_HEADS, DN_HEAD_DIM).astype(f32) for t in jnp.split(qkv, 3, axis=-1)]
    q = l2_normalize(q) * (DN_HEAD_DIM ** -0.5)
    k = l2_normalize(k)
    beta = jax.nn.sigmoid(b.astype(f32))
    g = -jnp.exp(a_log.astype(f32)) * jax.nn.softplus(a.astype(f32) + dt_bias.astype(f32))
    o = chunk_gated_delta(q, k, v, g, beta)
    o = rms_norm(o, norm_g) * jax.nn.silu(z.reshape(B, S, DN_HEADS, DN_HEAD_DIM).astype(f32))
    return o.reshape(B, S, DN_WIDTH).astype(qkv.dtype)


def spatial_gating_branch(sg_in, ln_g, ln_b, w_s, b_s):
    B, S, _ = sg_in.shape
    zz = jax.nn.gelu(sg_in, approximate=False)
    u, v = jnp.split(zz, 2, axis=-1)
    v = layer_norm(v, ln_g, ln_b)
    v = v.reshape(B, S // SG_BLOCK, SG_BLOCK, SG_GROUPS, SG_GROUP_DIM)
    chunk_id = jnp.arange(SG_BLOCK) // CHUNK
    mask = chunk_id[None, :] <= chunk_id[:, None]
    w = jnp.where(mask[None], w_s, 0.0)
    mixed = jnp.einsum("gij,bnjgc->bnigc", w.astype(v.dtype), v) + b_s.T[:, :, None].astype(v.dtype)
    return u * mixed.reshape(B, S, SG_WIDTH)


def mla_branch(cq, ckv, kr, cos, sin, cq_g, ckv_g, w_uq, w_ukv):
    B, S, _ = cq.shape
    q = (rms_norm(cq, cq_g) @ w_uq).reshape(B, S, MLA_HEADS, MLA_NOPE + MLA_ROPE)
    q_nope = q[..., :MLA_NOPE]
    q_rope = apply_rope(q[..., MLA_NOPE:], cos[:, :, None], sin[:, :, None])
    kv = (rms_norm(ckv, ckv_g) @ w_ukv).reshape(B, S, MLA_HEADS, MLA_NOPE + MLA_V)
    k_nope, v = kv[..., :MLA_NOPE], kv[..., MLA_NOPE:]
    k_rope = apply_rope(kr, cos, sin)
    nb = S // Q_BLOCK
    qn_blocks = q_nope.reshape(B, nb, Q_BLOCK, MLA_HEADS, MLA_NOPE).swapaxes(0, 1)
    qr_blocks = q_rope.reshape(B, nb, Q_BLOCK, MLA_HEADS, MLA_ROPE).swapaxes(0, 1)
    key_chunk = jnp.arange(S) // CHUNK
    scale = (MLA_NOPE + MLA_ROPE) ** -0.5

    def attend(args):
        blk, qn, qr = args
        q_chunk = (blk * Q_BLOCK + jnp.arange(Q_BLOCK)) // CHUNK
        s = (jnp.einsum("bqhd,bkhd->bhqk", qn, k_nope, preferred_element_type=jnp.float32)
             + jnp.einsum("bqhd,bkd->bhqk", qr, k_rope, preferred_element_type=jnp.float32)) * scale
        s = jnp.where(key_chunk[None, :] <= q_chunk[:, None], s, -1e30)
        p = jax.nn.softmax(s, axis=-1)
        return jnp.einsum("bhqk,bkhd->bqhd", p.astype(v.dtype), v)

    o = lax.map(attend, (jnp.arange(nb), qn_blocks, qr_blocks))
    return o.swapaxes(0, 1).reshape(B, S, MLA_HEADS * MLA_V)


def hybrid_mixer(h, cos, sin, w_in, conv_w, a_log, dt_bias, dn_g, ln_g, ln_b, sg_w, sg_b,
                 cq_g, ckv_g, w_uq, w_ukv, w_branch, w_out):
    B, S, D = h.shape
    proj = h @ w_in
    sizes = [3 * DN_WIDTH, DN_WIDTH, DN_HEADS, DN_HEADS, 2 * SG_WIDTH, MLA_Q_RANK, MLA_KV_RANK, MLA_ROPE]
    qkv, z, a, b, sg_in, cq, ckv, kr, gate_logits = jnp.split(proj, _offsets(sizes), axis=-1)
    o_a = deltanet_branch(qkv, z, a, b, conv_w, a_log, dt_bias, dn_g)
    o_b = spatial_gating_branch(sg_in, ln_g, ln_b, sg_w, sg_b)
    o_c = mla_branch(cq, ckv, kr, cos, sin, cq_g, ckv_g, w_uq, w_ukv)
    ys = jnp.stack([o_a, o_b, o_c], axis=2)
    y = jnp.einsum("bsnw,nwd->bsnd", ys, w_branch)
    gates = jax.nn.sigmoid(gate_logits.astype(jnp.float32)).reshape(B, S, N_BRANCH, D)
    merged = jnp.sum(gates.astype(y.dtype) * y, axis=2)
    return merged @ w_out


def setup_inputs(seed: int = 0) -> dict:
    key = jax.random.key(seed)
    ks = jax.random.split(key, 24)
    f32 = jnp.float32

    def nrm(k, shape, fan_in):
        return jax.random.normal(k, shape, f32) * (fan_in ** -0.5)

    def gain(k, shape):
        return 1.0 + 0.05 * jax.random.normal(k, shape, f32)

    x = jax.random.normal(ks[0], (BATCH, SEQ, D_MODEL), f32)
    offset = jax.random.randint(ks[1], (BATCH, 1), 0, 64, dtype=jnp.int32) * CHUNK
    positions = (offset + jnp.arange(SEQ, dtype=jnp.int32)[None, :]).astype(jnp.int32)
    dt = jnp.exp(jax.random.uniform(ks[11], (DEPTH, DN_HEADS), f32, math.log(1e-3), math.log(1e-1)))
    return {
        "x": x,
        "positions": positions,
        "norm_g": gain(ks[2], (DEPTH, 6, D_MODEL)),
        "ffn_w_gate": nrm(ks[3], (DEPTH, 2, D_MODEL, D_FF), D_MODEL),
        "ffn_w_up": nrm(ks[4], (DEPTH, 2, D_MODEL, D_FF), D_MODEL),
        "ffn_w_down": nrm(ks[5], (DEPTH, 2, D_FF, D_MODEL), D_FF),
        "w_in": nrm(ks[6], (DEPTH, D_MODEL, IN_COLS), D_MODEL),
        "dn_conv_w": nrm(ks[7], (DEPTH, DN_CONV, 3 * DN_WIDTH), DN_CONV),
        "dn_a_log": jnp.log(jax.random.uniform(ks[8], (DEPTH, DN_HEADS), f32, 1.0, 16.0)),
        "dn_dt_bias": dt + jnp.log(-jnp.expm1(-dt)),
        "dn_norm_g": gain(ks[9], (DEPTH, DN_HEAD_DIM)),
        "sg_ln_g": gain(ks[10], (DEPTH, SG_WIDTH)),
        "sg_ln_b": 0.02 * jax.random.normal(ks[12], (DEPTH, SG_WIDTH), f32),
        "sg_w": nrm(ks[13], (DEPTH, SG_GROUPS, SG_BLOCK, SG_BLOCK), SG_BLOCK),
        "sg_b": 1.0 + 0.1 * jax.random.normal(ks[14], (DEPTH, SG_GROUPS, SG_BLOCK), f32),
        "mla_cq_norm_g": gain(ks[15], (DEPTH, MLA_Q_RANK)),
        "mla_ckv_norm_g": gain(ks[16], (DEPTH, MLA_KV_RANK)),
        "mla_w_uq": nrm(ks[17], (DEPTH, MLA_Q_RANK, MLA_HEADS * (MLA_NOPE + MLA_ROPE)), MLA_Q_RANK),
        "mla_w_ukv": nrm(ks[18], (DEPTH, MLA_KV_RANK, MLA_HEADS * (MLA_NOPE + MLA_V)), MLA_KV_RANK),
        "w_branch": nrm(ks[19], (DEPTH, N_BRANCH, BRANCH_WIDTH, D_MODEL), BRANCH_WIDTH),
        "w_out": nrm(ks[20], (DEPTH, D_MODEL, D_MODEL), D_MODEL),
    }


def reference(x, positions, norm_g, ffn_w_gate, ffn_w_up, ffn_w_down, w_in, dn_conv_w, dn_a_log,
              dn_dt_bias, dn_norm_g, sg_ln_g, sg_ln_b, sg_w, sg_b, mla_cq_norm_g, mla_ckv_norm_g,
              mla_w_uq, mla_w_ukv, w_branch, w_out):
    inv_freq = jnp.power(ROPE_THETA, -jnp.arange(0, MLA_ROPE, 2, dtype=jnp.float32) / MLA_ROPE)
    ang = positions.astype(jnp.float32)[..., None] * inv_freq
    cos, sin = jnp.cos(ang), jnp.sin(ang)
    for l in range(DEPTH):
        ng = norm_g[l]
        f = swiglu(rms_norm(x, ng[0]), ffn_w_gate[l, 0], ffn_w_up[l, 0], ffn_w_down[l, 0])
        x = x + 0.5 * rms_norm(f, ng[1])
        h = rms_norm(x, ng[2])
        m = hybrid_mixer(h, cos, sin, w_in[l], dn_conv_w[l], dn_a_log[l], dn_dt_bias[l], dn_norm_g[l],
                         sg_ln_g[l], sg_ln_b[l], sg_w[l], sg_b[l], mla_cq_norm_g[l], mla_ckv_norm_g[l],
                         mla_w_uq[l], mla_w_ukv[l], w_branch[l], w_out[l])
        x = x + rms_norm(m, ng[3])
        f = swiglu(rms_norm(x, ng[4]), ffn_w_gate[l, 1], ffn_w_up[l, 1], ffn_w_down[l, 1])
        x = x + 0.5 * rms_norm(f, ng[5])
    return x
```

```python
import functools

import jax
import jax.numpy as jnp
from jax import lax
from jax.experimental import pallas as pl
from jax.experimental.pallas import tpu as pltpu

F32 = jnp.float32
BF16 = jnp.bfloat16

D_MODEL = 2048
DEPTH = 2
CHUNK = 64
NORM_EPS = 1e-6
DN_HEADS = 8
DN_HEAD_DIM = 128
DN_WIDTH = DN_HEADS * DN_HEAD_DIM
DN_CONV = 4
SG_GROUPS = 8
SG_GROUP_DIM = 128
SG_WIDTH = SG_GROUPS * SG_GROUP_DIM
SG_BLOCK = 128
MLA_HEADS = 8
MLA_Q_RANK = 512
MLA_KV_RANK = 512
MLA_NOPE = 128
MLA_ROPE = 64
MLA_V = 128
ROPE_THETA = 10000.0
N_BRANCH = 3
BRANCH_WIDTH = 1024
D_FF = 5632

LANES = 128
V7X_VMEM_BYTES = 64 * 1024 * 1024
VMEM_CAP = V7X_VMEM_BYTES - 6 * 1024 * 1024

OFF_QKV = 0
OFF_Z = OFF_QKV + 3 * DN_WIDTH
OFF_SG = OFF_Z + DN_WIDTH
OFF_GATE = OFF_SG + 2 * SG_WIDTH
OFF_CQ = OFF_GATE + N_BRANCH * D_MODEL
OFF_CKV = OFF_CQ + MLA_Q_RANK
OFF_AB = OFF_CKV + MLA_KV_RANK
OFF_KR = OFF_AB + LANES
PROJ_TILE = 512
PROJ_COLS = -(-(OFF_KR + LANES) // PROJ_TILE) * PROJ_TILE
MLA_QW = 2 * LANES


def _cparams(semantics, vmem_bytes):
    return pltpu.CompilerParams(dimension_semantics=semantics,
                                vmem_limit_bytes=int(min(max(vmem_bytes, 16 * 1024 * 1024), VMEM_CAP)))


def _dot(a, b):
    return jnp.dot(a, b, preferred_element_type=F32)


def _dot_nt(a, b):
    return lax.dot_general(a, b, (((1,), (1,)), ((), ())), preferred_element_type=F32)


def _dot_tn(a, b):
    return lax.dot_general(a, b, (((0,), (0,)), ((), ())), preferred_element_type=F32)


def _rms(x, eps=NORM_EPS):
    return x * lax.rsqrt(jnp.mean(x * x, axis=-1, keepdims=True) + eps)


def _silu(x):
    return x * jax.nn.sigmoid(x)


def _ffn_body(x_ref, gpre_ref, wg_ref, wu_ref, wd_ref, gpost_ref, o_ref, xn_ref, acc_ref, *, nf):
    f = pl.program_id(1)

    @pl.when(f == 0)
    def _():
        xn_ref[...] = (_rms(x_ref[...]) * gpre_ref[...]).astype(BF16)
        acc_ref[...] = jnp.zeros_like(acc_ref)

    xn = xn_ref[...]
    h = _dot(xn, wg_ref[...])
    u = _dot(xn, wu_ref[...])
    a = (_silu(h) * u).astype(BF16)
    acc_ref[...] += _dot(a, wd_ref[...])

    @pl.when(f == nf - 1)
    def _():
        o_ref[...] = x_ref[...] + 0.5 * (_rms(acc_ref[...]) * gpost_ref[...])


def _ffn(x, g_pre, w_gate, w_up, w_down, g_post, *, tm=512, tf=512):
    t, d = x.shape
    dff = w_gate.shape[1]
    nf = dff // tf
    vmem = (2 * 2 * tm * d * 4 + tm * d * 2 + tm * d * 4 + 2 * 3 * d * tf * 2 + 4 * tm * tf * 4 + 2 * tm * d * 4)
    return pl.pallas_call(
        functools.partial(_ffn_body, nf=nf),
        grid=(t // tm, nf),
        in_specs=[
            pl.BlockSpec((tm, d), lambda i, f: (i, 0)),
            pl.BlockSpec((1, d), lambda i, f: (0, 0)),
            pl.BlockSpec((d, tf), lambda i, f: (0, f)),
            pl.BlockSpec((d, tf), lambda i, f: (0, f)),
            pl.BlockSpec((tf, d), lambda i, f: (f, 0)),
            pl.BlockSpec((1, d), lambda i, f: (0, 0)),
        ],
        out_specs=pl.BlockSpec((tm, d), lambda i, f: (i, 0)),
        out_shape=jax.ShapeDtypeStruct((t, d), F32),
        scratch_shapes=[pltpu.VMEM((tm, d), BF16), pltpu.VMEM((tm, d), F32)],
        compiler_params=_cparams(("parallel", "arbitrary"), vmem),
        name="ffn",
    )(x, g_pre.reshape(1, d), w_gate, w_up, w_down, g_post.reshape(1, d))


def _proj_body(x_ref, g_ref, w_ref, o_ref, xn_ref):
    @pl.when(pl.program_id(1) == 0)
    def _():
        xn_ref[...] = (_rms(x_ref[...]) * g_ref[...]).astype(BF16)

    o_ref[...] = _dot(xn_ref[...], w_ref[...])


def _norm_proj(x, g, w, *, tm=1024, tn=PROJ_TILE):
    t, d = x.shape
    n = w.shape[1]
    tm = min(tm, t)
    vmem = 2 * tm * d * 4 + tm * d * 2 + 2 * d * tn * 2 + 3 * tm * tn * 4 + tm * d * 4
    return pl.pallas_call(
        _proj_body,
        grid=(t // tm, n // tn),
        in_specs=[
            pl.BlockSpec((tm, d), lambda i, j: (i, 0)),
            pl.BlockSpec((1, d), lambda i, j: (0, 0)),
            pl.BlockSpec((d, tn), lambda i, j: (0, j)),
        ],
        out_specs=pl.BlockSpec((tm, tn), lambda i, j: (i, j)),
        out_shape=jax.ShapeDtypeStruct((t, n), F32),
        scratch_shapes=[pltpu.VMEM((tm, d), BF16)],
        compiler_params=_cparams(("parallel", "arbitrary"), vmem),
        name="norm_proj",
    )(x, g.reshape(1, d), w)


def _rope_body(pos_ref, inv_ref, o_ref):
    ang = pos_ref[...] * inv_ref[...]
    lane = lax.broadcasted_iota(jnp.int32, ang.shape, 1)
    o_ref[...] = jnp.where(lane < 2 * (MLA_ROPE // 2), jnp.cos(ang), jnp.sin(ang))


def _rope_table(positions):
    b, s = positions.shape
    t = b * s
    inv_freq = jnp.power(ROPE_THETA, -jnp.arange(0, MLA_ROPE, 2, dtype=F32) / MLA_ROPE)
    inv = jnp.tile(inv_freq, LANES // (MLA_ROPE // 2)).reshape(1, LANES)
    pos = jnp.broadcast_to(positions.astype(F32).reshape(t, 1), (t, LANES))
    tm = min(1024, t)
    return pl.pallas_call(
        _rope_body,
        grid=(t // tm,),
        in_specs=[pl.BlockSpec((tm, LANES), lambda i: (i, 0)), pl.BlockSpec((1, LANES), lambda i: (0, 0))],
        out_specs=pl.BlockSpec((tm, LANES), lambda i: (i, 0)),
        out_shape=jax.ShapeDtypeStruct((t, LANES), F32),
        compiler_params=_cparams(("parallel",), 0),
        name="rope_table",
    )(pos, inv)


def _dn_body(alog_ref, dtb_ref, q_ref, k_ref, v_ref, z_ref, ab_ref, wq_ref, wk_ref, wv_ref, ng_ref, o_ref):
    h = pl.program_id(1)
    seq = q_ref.shape[1]
    lane = lax.broadcasted_iota(jnp.int32, (CHUNK, LANES), 1)
    row = lax.broadcasted_iota(jnp.int32, (CHUNK, CHUNK), 0)
    col = lax.broadcasted_iota(jnp.int32, (CHUNK, CHUNK), 1)
    incl = row >= col
    strict = row > col
    tril = incl.astype(F32)
    a_scale = -jnp.exp(jnp.full((1, 1), alog_ref[h], F32))
    dt_bias = dtb_ref[h]
    wq, wk, wv = wq_ref[...], wk_ref[...], wv_ref[...]
    ng = ng_ref[...]
    halo = 8

    def conv_silu(ref, w, c, c0):
        cur = ref[0, pl.ds(c0, CHUNK), :]
        p0 = pl.multiple_of(jnp.maximum(c0 - halo, 0), halo)
        prev = jnp.where(c > 0, ref[0, pl.ds(p0, halo), :], 0.0)
        xx = jnp.concatenate([prev, cur], axis=0)
        y = w[0:1] * xx[halo - 3:halo - 3 + CHUNK]
        for j in range(1, DN_CONV):
            y = y + w[j:j + 1] * xx[halo - 3 + j:halo - 3 + j + CHUNK]
        return _silu(y)

    def chunk_step(c, state):
        c0 = pl.multiple_of(c * CHUNK, CHUNK)
        q = conv_silu(q_ref, wq, c, c0)
        k = conv_silu(k_ref, wk, c, c0)
        v = conv_silu(v_ref, wv, c, c0)
        q = q * lax.rsqrt(jnp.sum(q * q, axis=-1, keepdims=True) + 1e-6) * (DN_HEAD_DIM ** -0.5)
        k = k * lax.rsqrt(jnp.sum(k * k, axis=-1, keepdims=True) + 1e-6)
        ab = ab_ref[0, pl.ds(c0, CHUNK), :]
        a_col = jnp.sum(jnp.where(lane == h, ab, 0.0), axis=-1, keepdims=True)
        b_col = jnp.sum(jnp.where(lane == h + DN_HEADS, ab, 0.0), axis=-1, keepdims=True)
        beta = jax.nn.sigmoid(b_col)
        xa = a_col + dt_bias
        g = a_scale * (jnp.maximum(xa, 0.0) + jnp.log1p(jnp.exp(-jnp.abs(xa))))
        gc = jnp.dot(tril, jnp.broadcast_to(g, (CHUNK, LANES)), precision=lax.Precision.HIGHEST,
                     preferred_element_type=F32)
        g_row = jnp.concatenate([gc, gc], axis=0).T[:CHUNK, :CHUNK]
        decay = jnp.exp(jnp.where(incl, gc[:, :CHUNK] - g_row, -jnp.inf))
        eg = jnp.exp(gc)
        kb = k * beta
        km = k.astype(BF16)
        a_mat = jnp.where(strict, _dot_nt(kb.astype(BF16), km) * decay, 0.0)
        p = -a_mat
        sol = jnp.concatenate([kb * eg, v * beta], axis=1)
        sol = sol + _dot(p.astype(BF16), sol.astype(BF16))
        for _ in range(5):
            pb = p.astype(BF16)
            p = _dot(pb, pb)
            sol = sol + _dot(p.astype(BF16), sol.astype(BF16))
        w = sol[:, :DN_HEAD_DIM]
        u = sol[:, DN_HEAD_DIM:]
        qk = _dot_nt(q.astype(BF16), km) * decay
        sb = state.astype(BF16)
        v_new = u - _dot(w.astype(BF16), sb)
        vnb = v_new.astype(BF16)
        o = _dot((q * eg).astype(BF16), sb) + _dot(qk.astype(BF16), vnb)
        g_last = gc[CHUNK - 1:CHUNK, :]
        kd = k * jnp.exp(g_last - gc)
        state = state * jnp.exp(g_last) + _dot_tn(kd.astype(BF16), vnb)
        z = z_ref[0, pl.ds(c0, CHUNK), :]
        o_ref[0, pl.ds(c0, CHUNK), :] = (_rms(o) * ng * _silu(z)).astype(o_ref.dtype)
        return state

    lax.fori_loop(0, seq // CHUNK, chunk_step, jnp.zeros((DN_HEAD_DIM, DN_HEAD_DIM), F32))


def _deltanet(proj3, conv_w, a_log, dt_bias, norm_g):
    b, s, _ = proj3.shape
    hd = DN_HEAD_DIM
    qb, kb, vb, zb, abb = (OFF_QKV // hd, (OFF_QKV + DN_WIDTH) // hd, (OFF_QKV + 2 * DN_WIDTH) // hd,
                           OFF_Z // hd, OFF_AB // hd)
    smem = pl.BlockSpec(memory_space=pltpu.SMEM)
    seq_spec = lambda base: pl.BlockSpec((1, s, hd), lambda i, h: (i, 0, base + h))
    w_spec = lambda base: pl.BlockSpec((DN_CONV, hd), lambda i, h: (0, base + h))
    vmem = 2 * 5 * s * hd * 4 + 2 * s * hd * 2 + 4 * 1024 * 1024
    return pl.pallas_call(
        _dn_body,
        grid=(b, DN_HEADS),
        in_specs=[smem, smem, seq_spec(qb), seq_spec(kb), seq_spec(vb), seq_spec(zb),
                  pl.BlockSpec((1, s, hd), lambda i, h: (i, 0, abb)),
                  w_spec(0), w_spec(DN_HEADS), w_spec(2 * DN_HEADS),
                  pl.BlockSpec((1, hd), lambda i, h: (0, 0))],
        out_specs=pl.BlockSpec((1, s, hd), lambda i, h: (i, 0, h)),
        out_shape=jax.ShapeDtypeStruct((b, s, DN_WIDTH), BF16),
        compiler_params=_cparams(("parallel", "parallel"), vmem),
        name="deltanet",
    )(a_log, dt_bias, proj3, proj3, proj3, proj3, proj3, conv_w, conv_w, conv_w, norm_g.reshape(1, hd))


def _gelu(x):
    return 0.5 * x * (1.0 + lax.erf(x * (0.5 ** 0.5)))


def _sg_body(u_ref, v_ref, lng_ref, lnb_ref, w_ref, bt_ref, o_ref):
    u = _gelu(u_ref[0])
    v = _gelu(v_ref[0])
    mu = jnp.mean(v, axis=-1, keepdims=True)
    vc = v - mu
    var = jnp.mean(vc * vc, axis=-1, keepdims=True)
    vn = (vc * lax.rsqrt(var + 1e-5) * lng_ref[...] + lnb_ref[...]).astype(BF16)
    row = lax.broadcasted_iota(jnp.int32, (SG_BLOCK, SG_BLOCK), 0)
    col = lax.broadcasted_iota(jnp.int32, (SG_BLOCK, SG_BLOCK), 1)
    mask = (col // CHUNK) <= (row // CHUNK)
    for g in range(SG_GROUPS):
        sl = slice(g * SG_GROUP_DIM, (g + 1) * SG_GROUP_DIM)
        wg = jnp.where(mask, w_ref[g], 0.0).astype(BF16)
        mixed = _dot(wg, vn[:, sl]) + bt_ref[:, g:g + 1]
        o_ref[0, :, sl] = (u[:, sl] * mixed).astype(o_ref.dtype)


def _gmlp(proj3, ln_g, ln_b, sg_w, sg_b):
    b, s, _ = proj3.shape
    ub = OFF_SG // SG_WIDTH
    vmem = 2 * 2 * SG_BLOCK * SG_WIDTH * 4 + 2 * SG_GROUPS * SG_BLOCK * SG_BLOCK * 4 + 8 * SG_BLOCK * SG_WIDTH * 4
    return pl.pallas_call(
        _sg_body,
        grid=(b, s // SG_BLOCK),
        in_specs=[
            pl.BlockSpec((1, SG_BLOCK, SG_WIDTH), lambda i, n: (i, n, ub)),
            pl.BlockSpec((1, SG_BLOCK, SG_WIDTH), lambda i, n: (i, n, ub + 1)),
            pl.BlockSpec((1, SG_WIDTH), lambda i, n: (0, 0)),
            pl.BlockSpec((1, SG_WIDTH), lambda i, n: (0, 0)),
            pl.BlockSpec((SG_GROUPS, SG_BLOCK, SG_BLOCK), lambda i, n: (0, 0, 0)),
            pl.BlockSpec((SG_BLOCK, SG_GROUPS), lambda i, n: (0, 0)),
        ],
        out_specs=pl.BlockSpec((1, SG_BLOCK, SG_WIDTH), lambda i, n: (i, n, 0)),
        out_shape=jax.ShapeDtypeStruct((b, s, SG_WIDTH), BF16),
        compiler_params=_cparams(("parallel", "parallel"), vmem),
        name="gmlp",
    )(proj3, proj3, ln_g.reshape(1, SG_WIDTH), ln_b.reshape(1, SG_WIDTH), sg_w, sg_b.T)


def _rotate(block, cs):
    t = block * cs
    return t + pltpu.roll(t, MLA_ROPE, 1)


def _mla_proj_body(cq_ref, ckv_ref, kr_ref, cs_ref, gq_ref, gkv_ref, wq_ref, wkv_ref, q_out, kv_out, kr_out):
    cs = cs_ref[...]
    scale = (MLA_NOPE + MLA_ROPE) ** -0.5
    cqn = (_rms(cq_ref[...]) * gq_ref[...]).astype(BF16)
    q = _dot(cqn, wq_ref[...])
    for h in range(MLA_HEADS):
        base = h * MLA_QW
        q_out[:, base:base + MLA_NOPE] = (q[:, base:base + MLA_NOPE] * scale).astype(q_out.dtype)
        q_out[:, base + MLA_NOPE:base + MLA_QW] = (_rotate(q[:, base + MLA_NOPE:base + MLA_QW], cs) * scale).astype(q_out.dtype)
    ckvn = (_rms(ckv_ref[...]) * gkv_ref[...]).astype(BF16)
    kv_out[...] = _dot(ckvn, wkv_ref[...]).astype(kv_out.dtype)
    lane = lax.broadcasted_iota(jnp.int32, cs.shape, 1)
    kr_out[...] = jnp.where(lane < MLA_ROPE, _rotate(kr_ref[...], cs), 0.0).astype(kr_out.dtype)


def _mla_proj(proj, cs, cq_g, ckv_g, w_uq, w_ukv, *, tm=512):
    t = proj.shape[0]
    tm = min(tm, t)
    nq = w_uq.shape[1]
    nkv = w_ukv.shape[1]
    vmem = (2 * tm * (MLA_Q_RANK + MLA_KV_RANK + 2 * LANES) * 4 + 2 * 2 * MLA_Q_RANK * (nq + nkv) * 2
            + 2 * tm * (nq + nkv + LANES) * 2 + 3 * tm * (nq + nkv) * 4)
    return pl.pallas_call(
        _mla_proj_body,
        grid=(t // tm,),
        in_specs=[
            pl.BlockSpec((tm, MLA_Q_RANK), lambda i: (i, OFF_CQ // MLA_Q_RANK)),
            pl.BlockSpec((tm, MLA_KV_RANK), lambda i: (i, OFF_CKV // MLA_KV_RANK)),
            pl.BlockSpec((tm, LANES), lambda i: (i, OFF_KR // LANES)),
            pl.BlockSpec((tm, LANES), lambda i: (i, 0)),
            pl.BlockSpec((1, MLA_Q_RANK), lambda i: (0, 0)),
            pl.BlockSpec((1, MLA_KV_RANK), lambda i: (0, 0)),
            pl.BlockSpec((MLA_Q_RANK, nq), lambda i: (0, 0)),
            pl.BlockSpec((MLA_KV_RANK, nkv), lambda i: (0, 0)),
        ],
        out_specs=[
            pl.BlockSpec((tm, nq), lambda i: (i, 0)),
            pl.BlockSpec((tm, nkv), lambda i: (i, 0)),
            pl.BlockSpec((tm, LANES), lambda i: (i, 0)),
        ],
        out_shape=[jax.ShapeDtypeStruct((t, nq), BF16), jax.ShapeDtypeStruct((t, nkv), BF16),
                   jax.ShapeDtypeStruct((t, LANES), BF16)],
        compiler_params=_cparams(("parallel",), vmem),
        name="mla_proj",
    )(proj, proj, proj, cs, cq_g.reshape(1, -1), ckv_g.reshape(1, -1), w_uq, w_ukv)


def _attn_body(q_ref, kn_ref, v_ref, kr_ref, o_ref, *, tq):
    i = pl.program_id(2)
    q = q_ref[0]
    qn = q[:, :MLA_NOPE]
    qr = q[:, MLA_NOPE:]
    q_chunk = (i * tq + lax.broadcasted_iota(jnp.int32, (tq, tq), 0)) // CHUNK
    k_iota = lax.broadcasted_iota(jnp.int32, (tq, tq), 1)

    def kv_step(j, carry):
        m, l, acc = carry
        k0 = pl.multiple_of(j * tq, tq)
        s = _dot_nt(qn, kn_ref[0, pl.ds(k0, tq), :]) + _dot_nt(qr, kr_ref[0, pl.ds(k0, tq), :])
        s = jnp.where((k0 + k_iota) // CHUNK <= q_chunk, s, -1e30)
        m_new = jnp.maximum(m, jnp.max(s, axis=-1, keepdims=True))
        alpha = jnp.exp(m - m_new)
        p = jnp.exp(s - m_new)
        l = alpha * l + jnp.sum(p, axis=-1, keepdims=True)
        acc = alpha * acc + _dot(p.astype(BF16), v_ref[0, pl.ds(k0, tq), :])
        return m_new, l, acc

    init = (jnp.full((tq, 1), -jnp.inf, F32), jnp.zeros((tq, 1), F32), jnp.zeros((tq, MLA_V), F32))
    _, l, acc = lax.fori_loop(0, i + 1, kv_step, init)
    o_ref[0] = (acc / l).astype(o_ref.dtype)


def _attention(q3, kv3, kr3, *, tq=256):
    b, s, _ = q3.shape
    tq = min(tq, s)
    vmem = 2 * tq * MLA_QW * 2 + 2 * 3 * s * LANES * 2 + 2 * tq * MLA_V * 2 + 8 * tq * tq * 4
    return pl.pallas_call(
        functools.partial(_attn_body, tq=tq),
        grid=(b, MLA_HEADS, s // tq),
        in_specs=[
            pl.BlockSpec((1, tq, MLA_QW), lambda bi, h, i: (bi, i, h)),
            pl.BlockSpec((1, s, MLA_NOPE), lambda bi, h, i: (bi, 0, 2 * h)),
            pl.BlockSpec((1, s, MLA_V), lambda bi, h, i: (bi, 0, 2 * h + 1)),
            pl.BlockSpec((1, s, LANES), lambda bi, h, i: (bi, 0, 0)),
        ],
        out_specs=pl.BlockSpec((1, tq, MLA_V), lambda bi, h, i: (bi, i, h)),
        out_shape=jax.ShapeDtypeStruct((b, s, MLA_HEADS * MLA_V), BF16),
        compiler_params=_cparams(("parallel", "parallel", "arbitrary"), vmem),
        name="mla_attention",
    )(q3, kv3, kv3, kr3)


def _merge_body(x_ref, oa_ref, ob_ref, oc_ref, ga_ref, gb_ref, gc_ref, wa_ref, wb_ref, wc_ref, wo_ref, g_ref,
                o_ref, acc_ref, *, nc):
    c = pl.program_id(1)

    @pl.when(c == 0)
    def _():
        acc_ref[...] = jnp.zeros_like(acc_ref)

    merged = jax.nn.sigmoid(ga_ref[...]) * _dot(oa_ref[...], wa_ref[0])
    merged = merged + jax.nn.sigmoid(gb_ref[...]) * _dot(ob_ref[...], wb_ref[0])
    merged = merged + jax.nn.sigmoid(gc_ref[...]) * _dot(oc_ref[...], wc_ref[0])
    acc_ref[...] += _dot(merged.astype(BF16), wo_ref[...])

    @pl.when(c == nc - 1)
    def _():
        o_ref[...] = x_ref[...] + _rms(acc_ref[...]) * g_ref[...]


def _merge(x, oa, ob, oc, proj, w_branch, w_out, g, *, tm=512, tc=512):
    t, d = x.shape
    tm = min(tm, t)
    nc = d // tc
    bw = oa.shape[1]
    gate_blk = OFF_GATE // tc
    vmem = (2 * 2 * tm * d * 4 + tm * d * 4 + 2 * 3 * tm * bw * 2 + 2 * 3 * tm * tc * 4 + 2 * 3 * bw * tc * 2
            + 2 * tc * d * 2 + 4 * tm * tc * 4 + tm * d * 4)
    o_spec = pl.BlockSpec((tm, bw), lambda i, c: (i, 0))
    gate_spec = lambda n: pl.BlockSpec((tm, tc), lambda i, c: (i, gate_blk + n * nc + c))
    wb_spec = lambda n: pl.BlockSpec((1, bw, tc), lambda i, c: (n, 0, c))
    return pl.pallas_call(
        functools.partial(_merge_body, nc=nc),
        grid=(t // tm, nc),
        in_specs=[pl.BlockSpec((tm, d), lambda i, c: (i, 0)), o_spec, o_spec, o_spec,
                  gate_spec(0), gate_spec(1), gate_spec(2), wb_spec(0), wb_spec(1), wb_spec(2),
                  pl.BlockSpec((tc, d), lambda i, c: (c, 0)),
                  pl.BlockSpec((1, d), lambda i, c: (0, 0))],
        out_specs=pl.BlockSpec((tm, d), lambda i, c: (i, 0)),
        out_shape=jax.ShapeDtypeStruct((t, d), F32),
        scratch_shapes=[pltpu.VMEM((tm, d), F32)],
        compiler_params=_cparams(("parallel", "arbitrary"), vmem),
        name="merge",
    )(x, oa, ob, oc, proj, proj, proj, w_branch, w_branch, w_branch, w_out, g.reshape(1, d))


def _swap_halves(w):
    half = w.shape[-1] // 2
    return jnp.concatenate([-w[..., half:], w[..., :half]], axis=-1)


def _relayout_w_in(w):
    d = w.shape[0]
    o = 0
    seg = {}
    for name, size in (("qkv", 3 * DN_WIDTH), ("z", DN_WIDTH), ("a", DN_HEADS), ("b", DN_HEADS), ("sg", 2 * SG_WIDTH),
                       ("cq", MLA_Q_RANK), ("ckv", MLA_KV_RANK), ("kr", MLA_ROPE), ("gate", N_BRANCH * D_MODEL)):
        seg[name] = w[:, o:o + size]
        o += size
    ab = jnp.concatenate([seg["a"], seg["b"], jnp.zeros((d, LANES - 2 * DN_HEADS), w.dtype)], axis=1)
    kr = jnp.concatenate([seg["kr"], _swap_halves(seg["kr"])], axis=1)
    cols = [seg["qkv"], seg["z"], seg["sg"], seg["gate"], seg["cq"], seg["ckv"], ab, kr]
    used = OFF_KR + LANES
    cols.append(jnp.zeros((d, PROJ_COLS - used), w.dtype))
    return jnp.concatenate(cols, axis=1).astype(BF16)


def _relayout_w_uq(w):
    r = w.shape[0]
    w = w.reshape(r, MLA_HEADS, MLA_NOPE + MLA_ROPE)
    rope = w[..., MLA_NOPE:]
    return jnp.concatenate([w[..., :MLA_NOPE], rope, _swap_halves(rope)], axis=-1).reshape(r, MLA_HEADS * MLA_QW).astype(BF16)


def kernel(x, positions, norm_g, ffn_w_gate, ffn_w_up, ffn_w_down, w_in, dn_conv_w, dn_a_log, dn_dt_bias, dn_norm_g, sg_ln_g, sg_ln_b, sg_w, sg_b, mla_cq_norm_g, mla_ckv_norm_g, mla_w_uq, mla_w_ukv, w_branch, w_out):
    b, s, d = x.shape
    t = b * s
    xf = x.reshape(t, d)
    cs = _rope_table(positions)
    for l in range(DEPTH):
        ng = norm_g[l]
        xf = _ffn(xf, ng[0], ffn_w_gate[l, 0].astype(BF16), ffn_w_up[l, 0].astype(BF16),
                  ffn_w_down[l, 0].astype(BF16), ng[1])
        proj = _norm_proj(xf, ng[2], _relayout_w_in(w_in[l]))
        proj3 = proj.reshape(b, s, PROJ_COLS)
        o_a = _deltanet(proj3, dn_conv_w[l], dn_a_log[l], dn_dt_bias[l], dn_norm_g[l])
        o_b = _gmlp(proj3, sg_ln_g[l], sg_ln_b[l], sg_w[l], sg_b[l])
        q, kv, kr = _mla_proj(proj, cs, mla_cq_norm_g[l], mla_ckv_norm_g[l], _relayout_w_uq(mla_w_uq[l]),
                              mla_w_ukv[l].astype(BF16))
        o_c = _attention(q.reshape(b, s, -1), kv.reshape(b, s, -1), kr.reshape(b, s, LANES))
        xf = _merge(xf, o_a.reshape(t, -1), o_b.reshape(t, -1), o_c.reshape(t, -1), proj,
                    w_branch[l].astype(BF16), w_out[l].astype(BF16), ng[3])
        xf = _ffn(xf, ng[4], ffn_w_gate[l, 1].astype(BF16), ffn_w_up[l, 1].astype(BF16),
                  ffn_w_down[l, 1].astype(BF16), ng[5])
    return xf.reshape(b, s, d)
```

```python
import functools

import jax
import jax.numpy as jnp
from jax import lax
from jax.experimental import pallas as pl
from jax.experimental.pallas import tpu as pltpu

F32 = jnp.float32
BF16 = jnp.bfloat16

D_MODEL = 2048
DEPTH = 2
CHUNK = 64
NORM_EPS = 1e-6
DN_HEADS = 8
DN_HEAD_DIM = 128
DN_WIDTH = DN_HEADS * DN_HEAD_DIM
DN_CONV = 4
SG_GROUPS = 8
SG_GROUP_DIM = 128
SG_WIDTH = SG_GROUPS * SG_GROUP_DIM
SG_BLOCK = 128
MLA_HEADS = 8
MLA_Q_RANK = 512
MLA_KV_RANK = 512
MLA_NOPE = 128
MLA_ROPE = 64
MLA_V = 128
ROPE_THETA = 10000.0
N_BRANCH = 3
BRANCH_WIDTH = 1024
D_FF = 5632

LANES = 128
SUBLANES = 8
V7X_VMEM_BYTES = 64 * 1024 * 1024
VMEM_CAP = V7X_VMEM_BYTES - 6 * 1024 * 1024

OFF_QKV = 0
OFF_Z = OFF_QKV + 3 * DN_WIDTH
OFF_SG = OFF_Z + DN_WIDTH
OFF_GATE = OFF_SG + 2 * SG_WIDTH
OFF_CQ = OFF_GATE + N_BRANCH * D_MODEL
OFF_CKV = OFF_CQ + MLA_Q_RANK
OFF_AB = OFF_CKV + MLA_KV_RANK
OFF_KR = OFF_AB + LANES
PROJ_TILE = 512
PROJ_COLS = -(-(OFF_KR + LANES) // PROJ_TILE) * PROJ_TILE
PROJ_TAIL = PROJ_COLS - OFF_AB
MLA_QW = 2 * LANES

SRC_AB = 4 * DN_WIDTH
SRC_SG = SRC_AB + 2 * DN_HEADS
SRC_CQ = SRC_SG + 2 * SG_WIDTH
SRC_CKV = SRC_CQ + MLA_Q_RANK
SRC_KR = SRC_CKV + MLA_KV_RANK
SRC_GATE = SRC_KR + MLA_ROPE
IN_COLS = SRC_GATE + N_BRANCH * D_MODEL


def _cparams(semantics, vmem_bytes):
    return pltpu.CompilerParams(dimension_semantics=semantics,
                                vmem_limit_bytes=int(min(max(vmem_bytes, 16 * 1024 * 1024), VMEM_CAP)))


def _dot(a, b):
    return jnp.dot(a, b, preferred_element_type=F32)


def _dot_nt(a, b):
    return lax.dot_general(a, b, (((1,), (1,)), ((), ())), preferred_element_type=F32)


def _dot_tn(a, b):
    return lax.dot_general(a, b, (((0,), (0,)), ((), ())), preferred_element_type=F32)


def _rms(x, eps=NORM_EPS):
    return x * lax.rsqrt(jnp.mean(x * x, axis=-1, keepdims=True) + eps)


def _silu(x):
    return x * jax.nn.sigmoid(x)


def _cast_body(w_ref, o_ref, *, nl):
    o_ref[...] = w_ref[(0,) * nl].astype(o_ref.dtype)


def _cast_bf16(w, lead=(), *, block_bytes=4 * 1024 * 1024):
    nl = len(lead)
    r, c = w.shape[nl:]
    rb = r
    while rb * c * 4 > block_bytes and rb % (4 * SUBLANES) == 0:
        rb //= 2
    return pl.pallas_call(
        functools.partial(_cast_body, nl=nl),
        grid=(r // rb,),
        in_specs=[pl.BlockSpec((1,) * nl + (rb, c), lambda i: tuple(lead) + (i, 0))],
        out_specs=pl.BlockSpec((rb, c), lambda i: (i, 0)),
        out_shape=jax.ShapeDtypeStruct((r, c), BF16),
        compiler_params=_cparams(("parallel",), 3 * 2 * rb * c * 4),
        name="cast_bf16",
    )(w)


def _relayout_body(w_ref, tail_ref, o_ref):
    w = w_ref[0]
    o_ref[:, OFF_QKV:OFF_SG] = w[:, :SRC_AB].astype(BF16)
    o_ref[:, OFF_SG:OFF_GATE] = w[:, SRC_SG:SRC_CQ].astype(BF16)
    o_ref[:, OFF_GATE:OFF_CQ] = w[:, SRC_GATE:IN_COLS].astype(BF16)
    o_ref[:, OFF_CQ:OFF_CKV] = w[:, SRC_CQ:SRC_CKV].astype(BF16)
    o_ref[:, OFF_CKV:OFF_AB] = w[:, SRC_CKV:SRC_KR].astype(BF16)
    o_ref[:, OFF_AB:PROJ_COLS] = tail_ref[...].astype(BF16)


def _swap_halves(w):
    half = w.shape[-1] // 2
    return jnp.concatenate([-w[..., half:], w[..., :half]], axis=-1)


def _relayout_w_in(w_in, l, *, rb=128):
    d = w_in.shape[1]
    w = w_in[l]
    kr = w[:, SRC_KR:SRC_GATE]
    tail = jnp.concatenate([w[:, SRC_AB:SRC_SG], jnp.zeros((d, LANES - 2 * DN_HEADS), F32), kr, _swap_halves(kr),
                            jnp.zeros((d, PROJ_TAIL - 2 * LANES), F32)], axis=1)
    return pl.pallas_call(
        _relayout_body,
        grid=(d // rb,),
        in_specs=[pl.BlockSpec((1, rb, IN_COLS), lambda i: (l, i, 0)),
                  pl.BlockSpec((rb, PROJ_TAIL), lambda i: (i, 0))],
        out_specs=pl.BlockSpec((rb, PROJ_COLS), lambda i: (i, 0)),
        out_shape=jax.ShapeDtypeStruct((d, PROJ_COLS), BF16),
        compiler_params=_cparams(("parallel",), 6 * rb * PROJ_COLS * 4),
        name="relayout_w_in",
    )(w_in, tail)


def _relayout_w_uq(w):
    r = w.shape[0]
    w = w.reshape(r, MLA_HEADS, MLA_NOPE + MLA_ROPE)
    rope = w[..., MLA_NOPE:]
    return jnp.concatenate([w[..., :MLA_NOPE], rope, _swap_halves(rope)], axis=-1).reshape(r, MLA_HEADS * MLA_QW).astype(BF16)


def _split_w_ukv(w):
    r = w.shape[0]
    w = w.reshape(r, MLA_HEADS, MLA_NOPE + MLA_V)
    w_uk = w[..., :MLA_NOPE].reshape(r, MLA_HEADS * MLA_NOPE).astype(BF16)
    w_uvt = w[..., MLA_NOPE:].reshape(r, MLA_HEADS * MLA_V).T.astype(BF16)
    return w_uk, w_uvt


def _ffn_body(x_ref, gpre_ref, wg_ref, wu_ref, wd_ref, gpost_ref, o_ref, xn_ref, acc_ref, *, nf):
    f = pl.program_id(1)

    @pl.when(f == 0)
    def _():
        xn_ref[...] = (_rms(x_ref[...]) * gpre_ref[...]).astype(BF16)
        acc_ref[...] = jnp.zeros_like(acc_ref)

    xn = xn_ref[...]
    h = _dot(xn, wg_ref[...])
    u = _dot(xn, wu_ref[...])
    a = (_silu(h) * u).astype(BF16)
    acc_ref[...] += _dot(a, wd_ref[...])

    @pl.when(f == nf - 1)
    def _():
        o_ref[...] = x_ref[...] + 0.5 * (_rms(acc_ref[...]) * gpost_ref[...])


def _ffn(x, g_pre, w_gate, w_up, w_down, g_post, *, tm=512, tf=512):
    t, d = x.shape
    dff = w_gate.shape[1]
    nf = dff // tf
    vmem = (2 * 2 * tm * d * 4 + tm * d * 2 + tm * d * 4 + 2 * 3 * d * tf * 2 + 4 * tm * tf * 4 + 2 * tm * d * 4)
    return pl.pallas_call(
        functools.partial(_ffn_body, nf=nf),
        grid=(t // tm, nf),
        in_specs=[
            pl.BlockSpec((tm, d), lambda i, f: (i, 0)),
            pl.BlockSpec((1, d), lambda i, f: (0, 0)),
            pl.BlockSpec((d, tf), lambda i, f: (0, f)),
            pl.BlockSpec((d, tf), lambda i, f: (0, f)),
            pl.BlockSpec((tf, d), lambda i, f: (f, 0)),
            pl.BlockSpec((1, d), lambda i, f: (0, 0)),
        ],
        out_specs=pl.BlockSpec((tm, d), lambda i, f: (i, 0)),
        out_shape=jax.ShapeDtypeStruct((t, d), F32),
        scratch_shapes=[pltpu.VMEM((tm, d), BF16), pltpu.VMEM((tm, d), F32)],
        compiler_params=_cparams(("parallel", "arbitrary"), vmem),
        name="ffn",
    )(x, g_pre.reshape(1, d), w_gate, w_up, w_down, g_post.reshape(1, d))


def _proj_body(x_ref, g_ref, w_ref, o_ref, xn_ref):
    @pl.when(pl.program_id(1) == 0)
    def _():
        xn_ref[...] = (_rms(x_ref[...]) * g_ref[...]).astype(BF16)

    o_ref[...] = _dot(xn_ref[...], w_ref[...])


def _norm_proj(x, g, w, *, tm=1024, tn=3 * PROJ_TILE):
    t, d = x.shape
    n = w.shape[1]
    tm = min(tm, t)
    vmem = tm * d * 4 + tm * d * 2 + 2 * d * tn * 2 + 3 * tm * tn * 4 + tm * d * 4
    return pl.pallas_call(
        _proj_body,
        grid=(t // tm, n // tn),
        in_specs=[
            pl.BlockSpec((tm, d), lambda i, j: (i, 0), pipeline_mode=pl.Buffered(1)),
            pl.BlockSpec((1, d), lambda i, j: (0, 0)),
            pl.BlockSpec((d, tn), lambda i, j: (0, j)),
        ],
        out_specs=pl.BlockSpec((tm, tn), lambda i, j: (i, j)),
        out_shape=jax.ShapeDtypeStruct((t, n), F32),
        scratch_shapes=[pltpu.VMEM((tm, d), BF16)],
        compiler_params=_cparams(("parallel", "arbitrary"), vmem),
        name="norm_proj",
    )(x, g.reshape(1, d), w)


def _rope_body(pos_ref, inv_ref, o_ref):
    ang = pos_ref[...] * inv_ref[...]
    lane = lax.broadcasted_iota(jnp.int32, ang.shape, 1)
    o_ref[...] = jnp.where(lane < 2 * (MLA_ROPE // 2), jnp.cos(ang), jnp.sin(ang))


def _rope_table(positions):
    b, s = positions.shape
    t = b * s
    inv_freq = jnp.power(ROPE_THETA, -jnp.arange(0, MLA_ROPE, 2, dtype=F32) / MLA_ROPE)
    inv = jnp.tile(inv_freq, LANES // (MLA_ROPE // 2)).reshape(1, LANES)
    pos = jnp.broadcast_to(positions.astype(F32).reshape(t, 1), (t, LANES))
    tm = min(1024, t)
    return pl.pallas_call(
        _rope_body,
        grid=(t // tm,),
        in_specs=[pl.BlockSpec((tm, LANES), lambda i: (i, 0)), pl.BlockSpec((1, LANES), lambda i: (0, 0))],
        out_specs=pl.BlockSpec((tm, LANES), lambda i: (i, 0)),
        out_shape=jax.ShapeDtypeStruct((t, LANES), F32),
        compiler_params=_cparams(("parallel",), 0),
        name="rope_table",
    )(pos, inv)


DN_HALO = SUBLANES


def _dn_body(alog_ref, dtb_ref, q_ref, k_ref, v_ref, z_ref, ab_ref, wq_ref, wk_ref, wv_ref, ng_ref, o_ref,
             state_ref, halo_ref):
    sblk = q_ref.shape[1]
    hd = DN_HEAD_DIM

    @pl.when(pl.program_id(1) == 0)
    def _():
        state_ref[...] = jnp.zeros_like(state_ref)
        halo_ref[...] = jnp.zeros_like(halo_ref)

    lane = lax.broadcasted_iota(jnp.int32, (1, LANES), 1)
    row = lax.broadcasted_iota(jnp.int32, (CHUNK, CHUNK), 0)
    col = lax.broadcasted_iota(jnp.int32, (CHUNK, CHUNK), 1)
    incl = row >= col
    strict = row > col
    tril = incl.astype(F32)
    a_scale = jnp.where(lane < DN_HEADS, -jnp.exp(alog_ref[...]), 0.0)
    dt_bias = dtb_ref[...]
    ng = ng_ref[...]
    seq_refs = (q_ref, k_ref, v_ref)
    w_refs = (wq_ref, wk_ref, wv_ref)

    def conv_silu(which, sl, c, c0):
        ref = seq_refs[which]
        cur = ref[0, pl.ds(c0, CHUNK), sl]
        p0 = pl.multiple_of(jnp.maximum(c0 - DN_HALO, 0), DN_HALO)
        prev = jnp.where(c > 0, ref[0, pl.ds(p0, DN_HALO), sl], halo_ref[which, :, sl])
        xx = jnp.concatenate([prev, cur], axis=0)
        w = w_refs[which][:, sl]
        first = DN_HALO - (DN_CONV - 1)
        y = w[0:1] * xx[first:first + CHUNK]
        for j in range(1, DN_CONV):
            y = y + w[j:j + 1] * xx[first + j:first + j + CHUNK]
        return _silu(y)

    def chunk_step(c, carry):
        c0 = pl.multiple_of(c * CHUNK, CHUNK)
        ab = ab_ref[0, pl.ds(c0, CHUNK), :]
        xa = ab + dt_bias
        g_all = a_scale * (jnp.maximum(xa, 0.0) + jnp.log1p(jnp.exp(-jnp.abs(xa))))
        beta_all = jax.nn.sigmoid(ab)
        gc_all = jnp.dot(tril, g_all, precision=lax.Precision.HIGHEST, preferred_element_type=F32)
        g_last = gc_all[CHUNK - 1:CHUNK, :]
        eg_all = jnp.exp(gc_all)
        ed_all = jnp.exp(g_last - gc_all)
        egl = jnp.exp(g_last)
        gc_t = jnp.concatenate([gc_all, jnp.zeros_like(gc_all)], axis=0).T
        heads = range(DN_HEADS)
        sls = [slice(h * hd, (h + 1) * hd) for h in heads]
        states = [state_ref[h] for h in heads]
        zs = [z_ref[0, pl.ds(c0, CHUNK), sls[h]] for h in heads]
        q = [conv_silu(0, sls[h], c, c0) for h in heads]
        k = [conv_silu(1, sls[h], c, c0) for h in heads]
        v = [conv_silu(2, sls[h], c, c0) for h in heads]
        q = [x * lax.rsqrt(jnp.sum(x * x, axis=-1, keepdims=True) + 1e-6) * (hd ** -0.5) for x in q]
        k = [x * lax.rsqrt(jnp.sum(x * x, axis=-1, keepdims=True) + 1e-6) for x in k]
        beta = [beta_all[:, DN_HEADS + h:DN_HEADS + h + 1] for h in heads]
        decay = [jnp.exp(jnp.where(incl, gc_all[:, h:h + 1] - gc_t[h:h + 1, :CHUNK], -jnp.inf)) for h in heads]
        kb = [k[h] * beta[h] for h in heads]
        km = [k[h].astype(BF16) for h in heads]
        kk = [_dot_nt(jnp.concatenate([kb[h], q[h]], axis=0).astype(BF16), km[h]) for h in heads]
        qk = [(kk[h][CHUNK:] * decay[h]).astype(BF16) for h in heads]
        p = [jnp.where(strict, -(kk[h][:CHUNK] * decay[h]), 0.0) for h in heads]
        sol = [jnp.concatenate([kb[h] * eg_all[:, h:h + 1], v[h] * beta[h]], axis=1) for h in heads]
        sol = [sol[h] + _dot(p[h].astype(BF16), sol[h].astype(BF16)) for h in heads]
        for _ in range(5):
            pb = [p[h].astype(BF16) for h in heads]
            p = [_dot(pb[h], pb[h]) for h in heads]
            sol = [sol[h] + _dot(p[h].astype(BF16), sol[h].astype(BF16)) for h in heads]
        qe = [q[h] * eg_all[:, h:h + 1] for h in heads]
        ws = [_dot(jnp.concatenate([sol[h][:, :hd], qe[h]], axis=0).astype(BF16), states[h].astype(BF16))
              for h in heads]
        v_new = [(sol[h][:, hd:] - ws[h][:CHUNK]).astype(BF16) for h in heads]
        o = [ws[h][CHUNK:] + _dot(qk[h], v_new[h]) for h in heads]
        kd = [(k[h] * ed_all[:, h:h + 1]).astype(BF16) for h in heads]
        new_states = [states[h] * egl[:, h:h + 1] + _dot_tn(kd[h], v_new[h]) for h in heads]
        for h in heads:
            o_ref[0, pl.ds(c0, CHUNK), sls[h]] = (_rms(o[h]) * ng * _silu(zs[h])).astype(o_ref.dtype)
        for h in heads:
            state_ref[h] = new_states[h]
        return carry

    lax.fori_loop(0, sblk // CHUNK, chunk_step, 0)
    for which in range(3):
        halo_ref[which] = seq_refs[which][0, sblk - DN_HALO:sblk, :]


def _deltanet(proj3, conv_w, a_log, dt_bias, norm_g, *, sblk=512):
    b, s, _ = proj3.shape
    sblk = min(sblk, s)
    wd = DN_WIDTH
    pad = lambda vec: jnp.concatenate([vec, jnp.zeros((LANES - DN_HEADS,), F32)]).reshape(1, LANES)
    seq_spec = lambda blk: pl.BlockSpec((1, sblk, wd), lambda i, j: (i, j, blk))
    w_spec = lambda blk: pl.BlockSpec((DN_CONV, wd), lambda i, j: (0, blk))
    vec_spec = pl.BlockSpec((1, LANES), lambda i, j: (0, 0))
    vmem = 2 * 4 * sblk * wd * 4 + 2 * sblk * wd * 2 + 2 * sblk * LANES * 4 + 8 * 1024 * 1024
    return pl.pallas_call(
        _dn_body,
        grid=(b, s // sblk),
        in_specs=[vec_spec, vec_spec,
                  seq_spec(OFF_QKV // wd), seq_spec(OFF_QKV // wd + 1), seq_spec(OFF_QKV // wd + 2), seq_spec(OFF_Z // wd),
                  pl.BlockSpec((1, sblk, LANES), lambda i, j: (i, j, OFF_AB // LANES)),
                  w_spec(0), w_spec(1), w_spec(2), vec_spec],
        out_specs=pl.BlockSpec((1, sblk, wd), lambda i, j: (i, j, 0)),
        out_shape=jax.ShapeDtypeStruct((b, s, wd), BF16),
        scratch_shapes=[pltpu.VMEM((DN_HEADS, DN_HEAD_DIM, DN_HEAD_DIM), F32), pltpu.VMEM((3, DN_HALO, wd), F32)],
        compiler_params=_cparams(("parallel", "arbitrary"), vmem),
        name="deltanet",
    )(pad(a_log), pad(dt_bias), proj3, proj3, proj3, proj3, proj3, conv_w, conv_w, conv_w, norm_g.reshape(1, -1))


def _gelu(x):
    return 0.5 * x * (1.0 + lax.erf(x * (0.5 ** 0.5)))


def _sg_body(u_ref, v_ref, lng_ref, lnb_ref, w_ref, bt_ref, o_ref):
    u = _gelu(u_ref[0])
    v = _gelu(v_ref[0])
    mu = jnp.mean(v, axis=-1, keepdims=True)
    vc = v - mu
    var = jnp.mean(vc * vc, axis=-1, keepdims=True)
    vn = (vc * lax.rsqrt(var + 1e-5) * lng_ref[...] + lnb_ref[...]).astype(BF16)
    row = lax.broadcasted_iota(jnp.int32, (SG_BLOCK, SG_BLOCK), 0)
    col = lax.broadcasted_iota(jnp.int32, (SG_BLOCK, SG_BLOCK), 1)
    mask = (col // CHUNK) <= (row // CHUNK)
    for g in range(SG_GROUPS):
        sl = slice(g * SG_GROUP_DIM, (g + 1) * SG_GROUP_DIM)
        wg = jnp.where(mask, w_ref[g], 0.0).astype(BF16)
        mixed = _dot(wg, vn[:, sl]) + bt_ref[:, g:g + 1]
        o_ref[0, :, sl] = (u[:, sl] * mixed).astype(o_ref.dtype)


def _gmlp(proj3, ln_g, ln_b, sg_w, sg_b):
    b, s, _ = proj3.shape
    ub = OFF_SG // SG_WIDTH
    vmem = 2 * 2 * SG_BLOCK * SG_WIDTH * 4 + 2 * SG_GROUPS * SG_BLOCK * SG_BLOCK * 4 + 8 * SG_BLOCK * SG_WIDTH * 4
    return pl.pallas_call(
        _sg_body,
        grid=(b, s // SG_BLOCK),
        in_specs=[
            pl.BlockSpec((1, SG_BLOCK, SG_WIDTH), lambda i, n: (i, n, ub)),
            pl.BlockSpec((1, SG_BLOCK, SG_WIDTH), lambda i, n: (i, n, ub + 1)),
            pl.BlockSpec((1, SG_WIDTH), lambda i, n: (0, 0)),
            pl.BlockSpec((1, SG_WIDTH), lambda i, n: (0, 0)),
            pl.BlockSpec((SG_GROUPS, SG_BLOCK, SG_BLOCK), lambda i, n: (0, 0, 0)),
            pl.BlockSpec((SG_BLOCK, SG_GROUPS), lambda i, n: (0, 0)),
        ],
        out_specs=pl.BlockSpec((1, SG_BLOCK, SG_WIDTH), lambda i, n: (i, n, 0)),
        out_shape=jax.ShapeDtypeStruct((b, s, SG_WIDTH), BF16),
        compiler_params=_cparams(("parallel", "parallel"), vmem),
        name="gmlp",
    )(proj3, proj3, ln_g.reshape(1, SG_WIDTH), ln_b.reshape(1, SG_WIDTH), sg_w, sg_b.T)


def _rotate(block, cs):
    t = block * cs
    return t + pltpu.roll(t, MLA_ROPE, 1)


def _mla_proj_body(cq_ref, ckv_ref, kr_ref, cs_ref, gq_ref, gkv_ref, wq_ref, wk_ref, wvt_ref, q_out, k_out, vt_out):
    cs = cs_ref[0]
    scale = (MLA_NOPE + MLA_ROPE) ** -0.5
    cqn = (_rms(cq_ref[0]) * gq_ref[...]).astype(BF16)
    q = _dot(cqn, wq_ref[...])
    lane = lax.broadcasted_iota(jnp.int32, cs.shape, 1)
    k_rope = jnp.where(lane < MLA_ROPE, _rotate(kr_ref[0], cs), 0.0).astype(k_out.dtype)
    ckvn = (_rms(ckv_ref[0]) * gkv_ref[...]).astype(BF16)
    k = _dot(ckvn, wk_ref[...])
    for h in range(MLA_HEADS):
        base = h * MLA_QW
        q_out[0, :, base:base + MLA_NOPE] = (q[:, base:base + MLA_NOPE] * scale).astype(q_out.dtype)
        q_out[0, :, base + MLA_NOPE:base + MLA_QW] = (_rotate(q[:, base + MLA_NOPE:base + MLA_QW], cs) * scale).astype(q_out.dtype)
        k_out[0, :, base:base + MLA_NOPE] = k[:, h * MLA_NOPE:(h + 1) * MLA_NOPE].astype(k_out.dtype)
        k_out[0, :, base + MLA_NOPE:base + MLA_QW] = k_rope
    vt_out[0] = _dot_nt(wvt_ref[...], ckvn).astype(vt_out.dtype)


def _mla_proj(proj3, cs3, cq_g, ckv_g, w_uq, w_uk, w_uvt, *, tm=512):
    b, s, _ = proj3.shape
    tm = min(tm, s)
    nq = w_uq.shape[1]
    nk = MLA_HEADS * MLA_QW
    nv = w_uvt.shape[0]
    vmem = (2 * tm * (MLA_Q_RANK + MLA_KV_RANK + 2 * LANES) * 4 + 2 * MLA_Q_RANK * (nq + w_uk.shape[1] + nv) * 2
            + 2 * tm * (nq + nk + nv) * 2 + 3 * tm * (nq + nk + nv) * 4)
    const = lambda shape: pl.BlockSpec(shape, lambda bi, i: (0, 0))
    return pl.pallas_call(
        _mla_proj_body,
        grid=(b, s // tm),
        in_specs=[
            pl.BlockSpec((1, tm, MLA_Q_RANK), lambda bi, i: (bi, i, OFF_CQ // MLA_Q_RANK)),
            pl.BlockSpec((1, tm, MLA_KV_RANK), lambda bi, i: (bi, i, OFF_CKV // MLA_KV_RANK)),
            pl.BlockSpec((1, tm, LANES), lambda bi, i: (bi, i, OFF_KR // LANES)),
            pl.BlockSpec((1, tm, LANES), lambda bi, i: (bi, i, 0)),
            const((1, MLA_Q_RANK)), const((1, MLA_KV_RANK)),
            const(w_uq.shape), const(w_uk.shape), const(w_uvt.shape),
        ],
        out_specs=[
            pl.BlockSpec((1, tm, nq), lambda bi, i: (bi, i, 0)),
            pl.BlockSpec((1, tm, nk), lambda bi, i: (bi, i, 0)),
            pl.BlockSpec((1, nv, tm), lambda bi, i: (bi, 0, i)),
        ],
        out_shape=[jax.ShapeDtypeStruct((b, s, nq), BF16), jax.ShapeDtypeStruct((b, s, nk), BF16),
                   jax.ShapeDtypeStruct((b, nv, s), BF16)],
        compiler_params=_cparams(("parallel", "parallel"), vmem),
        name="mla_proj",
    )(proj3, proj3, proj3, cs3, cq_g.reshape(1, -1), ckv_g.reshape(1, -1), w_uq, w_uk, w_uvt)


ATTN_HEADS_PER_STEP = 8


def _attn_body(q_ref, k_ref, vt_ref, o_ref, *, tq, nh):
    i = pl.program_id(2)
    diag_mask = (lax.broadcasted_iota(jnp.int32, (tq, tq), 0) // CHUNK
                 <= lax.broadcasted_iota(jnp.int32, (tq, tq), 1) // CHUNK)

    def kv_step(j, carry, masked):
        k0 = pl.multiple_of(j * tq, tq)
        heads = range(nh)
        st = [_dot_nt(k_ref[0, pl.ds(k0, tq), h * MLA_QW:(h + 1) * MLA_QW], q_ref[0, :, h * MLA_QW:(h + 1) * MLA_QW])
              for h in heads]
        if masked:
            st = [jnp.where(diag_mask, x, -1e30) for x in st]
        m_new = [jnp.maximum(carry[h][0], jnp.max(st[h], axis=0, keepdims=True)) for h in heads]
        alpha = [jnp.exp(carry[h][0] - m_new[h]) for h in heads]
        pt = [jnp.exp(st[h] - m_new[h]) for h in heads]
        l = [alpha[h] * carry[h][1] + jnp.sum(pt[h], axis=0, keepdims=True) for h in heads]
        pv = [_dot(vt_ref[0, h * MLA_V:(h + 1) * MLA_V, pl.ds(k0, tq)], pt[h].astype(BF16)) for h in heads]
        return tuple((m_new[h], l[h], alpha[h] * carry[h][2] + pv[h]) for h in heads)

    init = tuple((jnp.full((1, tq), -jnp.inf, F32), jnp.zeros((1, tq), F32), jnp.zeros((MLA_V, tq), F32))
                 for _ in range(nh))
    carry = lax.fori_loop(0, i, functools.partial(kv_step, masked=False), init)
    carry = kv_step(i, carry, masked=True)
    for h in range(nh):
        _, l, acc = carry[h]
        o_ref[0, :, h * MLA_V:(h + 1) * MLA_V] = (acc / l).T.astype(o_ref.dtype)


def _attention(q3, k3, vt3, *, tq=256, nh=ATTN_HEADS_PER_STEP):
    b, s, _ = q3.shape
    tq = min(tq, s)
    vmem = (2 * tq * nh * MLA_QW * 2 + 2 * s * nh * MLA_QW * 2 + 2 * s * nh * MLA_V * 2 + 2 * tq * nh * MLA_V * 2
            + nh * 8 * tq * tq * 4)
    return pl.pallas_call(
        functools.partial(_attn_body, tq=tq, nh=nh),
        grid=(b, MLA_HEADS // nh, s // tq),
        in_specs=[
            pl.BlockSpec((1, tq, nh * MLA_QW), lambda bi, hg, i: (bi, i, hg)),
            pl.BlockSpec((1, s, nh * MLA_QW), lambda bi, hg, i: (bi, 0, hg)),
            pl.BlockSpec((1, nh * MLA_V, s), lambda bi, hg, i: (bi, hg, 0)),
        ],
        out_specs=pl.BlockSpec((1, tq, nh * MLA_V), lambda bi, hg, i: (bi, i, hg)),
        out_shape=jax.ShapeDtypeStruct((b, s, MLA_HEADS * MLA_V), BF16),
        compiler_params=_cparams(("parallel", "parallel", "arbitrary"), vmem),
        name="mla_attention",
    )(q3, k3, vt3)


def _merge_body(x_ref, oa_ref, ob_ref, oc_ref, ga_ref, gb_ref, gc_ref, wa_ref, wb_ref, wc_ref, wo_ref, g_ref,
                o_ref, acc_ref, *, nc):
    c = pl.program_id(1)

    @pl.when(c == 0)
    def _():
        acc_ref[...] = jnp.zeros_like(acc_ref)

    merged = jax.nn.sigmoid(ga_ref[...]) * _dot(oa_ref[...], wa_ref[...])
    merged = merged + jax.nn.sigmoid(gb_ref[...]) * _dot(ob_ref[...], wb_ref[...])
    merged = merged + jax.nn.sigmoid(gc_ref[...]) * _dot(oc_ref[...], wc_ref[...])
    acc_ref[...] += _dot(merged.astype(BF16), wo_ref[...])

    @pl.when(c == nc - 1)
    def _():
        o_ref[...] = x_ref[...] + _rms(acc_ref[...]) * g_ref[...]


def _merge(x, oa, ob, oc, proj, w_branch, w_out, g, *, tm=512, tc=512):
    t, d = x.shape
    tm = min(tm, t)
    nc = d // tc
    bw = oa.shape[1]
    gate_blk = OFF_GATE // tc
    vmem = (2 * 2 * tm * d * 4 + tm * d * 4 + 2 * 3 * tm * bw * 2 + 2 * 3 * tm * tc * 4 + 2 * 3 * bw * tc * 2
            + 2 * tc * d * 2 + 4 * tm * tc * 4 + tm * d * 4)
    o_spec = pl.BlockSpec((tm, bw), lambda i, c: (i, 0))
    gate_spec = lambda n: pl.BlockSpec((tm, tc), lambda i, c: (i, gate_blk + n * nc + c))
    wb_spec = lambda n: pl.BlockSpec((bw, tc), lambda i, c: (n, c))
    return pl.pallas_call(
        functools.partial(_merge_body, nc=nc),
        grid=(t // tm, nc),
        in_specs=[pl.BlockSpec((tm, d), lambda i, c: (i, 0)), o_spec, o_spec, o_spec,
                  gate_spec(0), gate_spec(1), gate_spec(2), wb_spec(0), wb_spec(1), wb_spec(2),
                  pl.BlockSpec((tc, d), lambda i, c: (c, 0)),
                  pl.BlockSpec((1, d), lambda i, c: (0, 0))],
        out_specs=pl.BlockSpec((tm, d), lambda i, c: (i, 0)),
        out_shape=jax.ShapeDtypeStruct((t, d), F32),
        scratch_shapes=[pltpu.VMEM((tm, d), F32)],
        compiler_params=_cparams(("parallel", "arbitrary"), vmem),
        name="merge",
    )(x, oa, ob, oc, proj, proj, proj, w_branch, w_branch, w_branch, w_out, g.reshape(1, d))


def kernel(x, positions, norm_g, ffn_w_gate, ffn_w_up, ffn_w_down, w_in, dn_conv_w, dn_a_log, dn_dt_bias, dn_norm_g, sg_ln_g, sg_ln_b, sg_w, sg_b, mla_cq_norm_g, mla_ckv_norm_g, mla_w_uq, mla_w_ukv, w_branch, w_out):
    b, s, d = x.shape
    t = b * s
    xf = x.reshape(t, d)
    cs = _rope_table(positions)
    w_branch2 = w_branch.reshape(DEPTH, N_BRANCH * BRANCH_WIDTH, d)
    for l in range(DEPTH):
        ng = norm_g[l]
        xf = _ffn(xf, ng[0], _cast_bf16(ffn_w_gate, (l, 0)), _cast_bf16(ffn_w_up, (l, 0)),
                  _cast_bf16(ffn_w_down, (l, 0)), ng[1])
        proj = _norm_proj(xf, ng[2], _relayout_w_in(w_in, l))
        proj3 = proj.reshape(b, s, PROJ_COLS)
        o_a = _deltanet(proj3, dn_conv_w[l], dn_a_log[l], dn_dt_bias[l], dn_norm_g[l])
        o_b = _gmlp(proj3, sg_ln_g[l], sg_ln_b[l], sg_w[l], sg_b[l])
        w_uk, w_uvt = _split_w_ukv(mla_w_ukv[l])
        q, k, vt = _mla_proj(proj3, cs.reshape(b, s, LANES), mla_cq_norm_g[l], mla_ckv_norm_g[l],
                             _relayout_w_uq(mla_w_uq[l]), w_uk, w_uvt)
        o_c = _attention(q, k, vt)
        xf = _merge(xf, o_a.reshape(t, -1), o_b.reshape(t, -1), o_c.reshape(t, -1), proj,
                    _cast_bf16(w_branch2, (l,)), _cast_bf16(w_out, (l,)), ng[3])
        xf = _ffn(xf, ng[4], _cast_bf16(ffn_w_gate, (l, 1)), _cast_bf16(ffn_w_up, (l, 1)),
                  _cast_bf16(ffn_w_down, (l, 1)), ng[5])
    return xf.reshape(b, s, d)
```

```python
import functools

import jax
import jax.numpy as jnp
from jax import lax
from jax.experimental import pallas as pl
from jax.experimental.pallas import tpu as pltpu

F32 = jnp.float32
BF16 = jnp.bfloat16

D_MODEL = 2048
DEPTH = 2
CHUNK = 64
NORM_EPS = 1e-6
DN_HEADS = 8
DN_HEAD_DIM = 128
DN_WIDTH = DN_HEADS * DN_HEAD_DIM
DN_CONV = 4
SG_GROUPS = 8
SG_GROUP_DIM = 128
SG_WIDTH = SG_GROUPS * SG_GROUP_DIM
SG_BLOCK = 128
MLA_HEADS = 8
MLA_Q_RANK = 512
MLA_KV_RANK = 512
MLA_NOPE = 128
MLA_ROPE = 64
MLA_V = 128
ROPE_THETA = 10000.0
N_BRANCH = 3
BRANCH_WIDTH = 1024
D_FF = 5632

LANES = 128
SUBLANES = 8
V7X_VMEM_BYTES = 64 * 1024 * 1024
VMEM_CAP = V7X_VMEM_BYTES - 6 * 1024 * 1024

OFF_QKV = 0
OFF_Z = OFF_QKV + 3 * DN_WIDTH
OFF_SG = OFF_Z + DN_WIDTH
OFF_GATE = OFF_SG + 2 * SG_WIDTH
OFF_CQ = OFF_GATE + N_BRANCH * D_MODEL
OFF_CKV = OFF_CQ + MLA_Q_RANK
OFF_AB = OFF_CKV + MLA_KV_RANK
OFF_KR = OFF_AB + LANES
PROJ_TILE = 512
PROJ_COLS = -(-(OFF_KR + LANES) // PROJ_TILE) * PROJ_TILE
PROJ_TAIL = PROJ_COLS - OFF_AB
MLA_QW = 2 * LANES

SRC_AB = 4 * DN_WIDTH
SRC_SG = SRC_AB + 2 * DN_HEADS
SRC_CQ = SRC_SG + 2 * SG_WIDTH
SRC_CKV = SRC_CQ + MLA_Q_RANK
SRC_KR = SRC_CKV + MLA_KV_RANK
SRC_GATE = SRC_KR + MLA_ROPE
IN_COLS = SRC_GATE + N_BRANCH * D_MODEL


def _cparams(semantics, vmem_bytes):
    return pltpu.CompilerParams(dimension_semantics=semantics,
                                vmem_limit_bytes=int(min(max(vmem_bytes, 16 * 1024 * 1024), VMEM_CAP)))


def _dot(a, b):
    return jnp.dot(a, b, preferred_element_type=F32)


def _dot_nt(a, b):
    return lax.dot_general(a, b, (((1,), (1,)), ((), ())), preferred_element_type=F32)


def _dot_tn(a, b):
    return lax.dot_general(a, b, (((0,), (0,)), ((), ())), preferred_element_type=F32)


def _rms(x, eps=NORM_EPS):
    return x * lax.rsqrt(jnp.mean(x * x, axis=-1, keepdims=True) + eps)


def _silu(x):
    return x * jax.nn.sigmoid(x)


def _cast_body(w_ref, o_ref, *, nl):
    o_ref[...] = w_ref[(0,) * nl].astype(o_ref.dtype)


def _cast_bf16(w, lead=(), *, block_bytes=4 * 1024 * 1024):
    nl = len(lead)
    r, c = w.shape[nl:]
    rb = r
    while rb * c * 4 > block_bytes and rb % (4 * SUBLANES) == 0:
        rb //= 2
    return pl.pallas_call(
        functools.partial(_cast_body, nl=nl),
        grid=(r // rb,),
        in_specs=[pl.BlockSpec((1,) * nl + (rb, c), lambda i: tuple(lead) + (i, 0))],
        out_specs=pl.BlockSpec((rb, c), lambda i: (i, 0)),
        out_shape=jax.ShapeDtypeStruct((r, c), BF16),
        compiler_params=_cparams(("parallel",), 3 * 2 * rb * c * 4),
        name="cast_bf16",
    )(w)


_WIDE_SEGMENTS = ((OFF_QKV, 0, SRC_AB), (OFF_SG, SRC_SG, 2 * SG_WIDTH), (OFF_GATE, SRC_GATE, N_BRANCH * D_MODEL),
                  (OFF_CQ, SRC_CQ, MLA_Q_RANK), (OFF_CKV, SRC_CKV, MLA_KV_RANK))


def _relayout_body(wt_ref, tail_ref, o_ref, *, n_wide):
    i = pl.program_id(0)

    @pl.when(i < n_wide)
    def _():
        o_ref[...] = wt_ref[0].T.astype(BF16)

    @pl.when(i >= n_wide)
    def _():
        o_ref[...] = tail_ref[...].astype(BF16)


def _swap_halves(w):
    half = w.shape[-1] // 2
    return jnp.concatenate([-w[..., half:], w[..., :half]], axis=-1)


def _relayout_w_in(w_in, l):
    d = w_in.shape[1]
    tb = PROJ_TAIL
    kr = w_in[l, :, SRC_KR:SRC_GATE]
    tail = jnp.concatenate([w_in[l, :, SRC_AB:SRC_SG], jnp.zeros((d, LANES - 2 * DN_HEADS), F32), kr, _swap_halves(kr),
                            jnp.zeros((d, PROJ_TAIL - 2 * LANES), F32)], axis=1)
    n_wide = OFF_AB // tb

    def src_start(i):
        dst = i * tb
        start = 0
        for d0, s0, width in _WIDE_SEGMENTS:
            start = jnp.where((dst >= d0) & (dst < d0 + width), s0 + dst - d0, start)
        return pl.multiple_of(start, 2 * SUBLANES)

    return pl.pallas_call(
        functools.partial(_relayout_body, n_wide=n_wide),
        grid=(PROJ_COLS // tb,),
        in_specs=[pl.BlockSpec((pl.Element(1), pl.Element(tb), pl.Element(d)), lambda i: (l, src_start(i), 0)),
                  pl.BlockSpec((d, tb), lambda i: (0, 0))],
        out_specs=pl.BlockSpec((d, tb), lambda i: (0, i)),
        out_shape=jax.ShapeDtypeStruct((d, PROJ_COLS), BF16),
        compiler_params=_cparams(("parallel",), 2 * tb * d * 4 + 2 * tb * d * 4 + 2 * tb * d * 2 + 2 * tb * d * 4),
        name="relayout_w_in",
    )(jnp.swapaxes(w_in, 1, 2), tail)


def _relayout_w_uq(w):
    r = w.shape[0]
    w = w.reshape(r, MLA_HEADS, MLA_NOPE + MLA_ROPE)
    rope = w[..., MLA_NOPE:]
    return jnp.concatenate([w[..., :MLA_NOPE], rope, _swap_halves(rope)], axis=-1).reshape(r, MLA_HEADS * MLA_QW).astype(BF16)


def _split_w_ukv(w):
    r = w.shape[0]
    w = w.reshape(r, MLA_HEADS, MLA_NOPE + MLA_V)
    w_uk = w[..., :MLA_NOPE].reshape(r, MLA_HEADS * MLA_NOPE).astype(BF16)
    w_uvt = w[..., MLA_NOPE:].reshape(r, MLA_HEADS * MLA_V).T.astype(BF16)
    return w_uk, w_uvt


def _ffn_body(x_ref, gpre_ref, wg_ref, wu_ref, wd_ref, gpost_ref, o_ref, xn_ref, *, nf):
    f = pl.program_id(1)

    @pl.when(f == 0)
    def _():
        xn_ref[...] = (_rms(x_ref[...]) * gpre_ref[...]).astype(BF16)
        o_ref[...] = jnp.zeros_like(o_ref)

    xn = xn_ref[...]
    h = _dot(xn, wg_ref[...])
    u = _dot(xn, wu_ref[...])
    a = (_silu(h) * u).astype(BF16)
    o_ref[...] += _dot(a, wd_ref[...])

    @pl.when(f == nf - 1)
    def _():
        o_ref[...] = x_ref[...] + 0.5 * (_rms(o_ref[...]) * gpost_ref[...])


def _ffn(x, g_pre, w_gate, w_up, w_down, g_post, *, tm=1024, tf=512):
    t, d = x.shape
    tm = min(tm, t)
    dff = w_gate.shape[1]
    nf = dff // tf
    vmem = (tm * d * 4 + 2 * tm * d * 4 + tm * d * 2 + 2 * 3 * d * tf * 2 + 5 * tm * tf * 4 + tm * d * 4)
    return pl.pallas_call(
        functools.partial(_ffn_body, nf=nf),
        grid=(t // tm, nf),
        in_specs=[
            pl.BlockSpec((tm, d), lambda i, f: (i, 0), pipeline_mode=pl.Buffered(1)),
            pl.BlockSpec((1, d), lambda i, f: (0, 0)),
            pl.BlockSpec((d, tf), lambda i, f: (0, f)),
            pl.BlockSpec((d, tf), lambda i, f: (0, f)),
            pl.BlockSpec((tf, d), lambda i, f: (f, 0)),
            pl.BlockSpec((1, d), lambda i, f: (0, 0)),
        ],
        out_specs=pl.BlockSpec((tm, d), lambda i, f: (i, 0)),
        out_shape=jax.ShapeDtypeStruct((t, d), F32),
        scratch_shapes=[pltpu.VMEM((tm, d), BF16)],
        compiler_params=_cparams(("parallel", "arbitrary"), vmem),
        name="ffn",
    )(x, g_pre.reshape(1, d), w_gate, w_up, w_down, g_post.reshape(1, d))


def _proj_body(x_ref, g_ref, w_ref, o_ref, xn_ref):
    @pl.when(pl.program_id(1) == 0)
    def _():
        xn_ref[...] = (_rms(x_ref[...]) * g_ref[...]).astype(BF16)

    o_ref[...] = _dot(xn_ref[...], w_ref[...])


def _norm_proj(x, g, w, *, tm=1024, tn=3 * PROJ_TILE):
    t, d = x.shape
    n = w.shape[1]
    tm = min(tm, t)
    vmem = tm * d * 4 + tm * d * 2 + 2 * d * tn * 2 + 3 * tm * tn * 4 + tm * d * 4
    return pl.pallas_call(
        _proj_body,
        grid=(t // tm, n // tn),
        in_specs=[
            pl.BlockSpec((tm, d), lambda i, j: (i, 0), pipeline_mode=pl.Buffered(1)),
            pl.BlockSpec((1, d), lambda i, j: (0, 0)),
            pl.BlockSpec((d, tn), lambda i, j: (0, j)),
        ],
        out_specs=pl.BlockSpec((tm, tn), lambda i, j: (i, j)),
        out_shape=jax.ShapeDtypeStruct((t, n), F32),
        scratch_shapes=[pltpu.VMEM((tm, d), BF16)],
        compiler_params=_cparams(("parallel", "arbitrary"), vmem),
        name="norm_proj",
    )(x, g.reshape(1, d), w)


def _rope_body(pos_ref, inv_ref, o_ref):
    ang = pos_ref[...] * inv_ref[...]
    lane = lax.broadcasted_iota(jnp.int32, ang.shape, 1)
    o_ref[...] = jnp.where(lane < 2 * (MLA_ROPE // 2), jnp.cos(ang), jnp.sin(ang))


def _rope_table(positions):
    b, s = positions.shape
    t = b * s
    inv_freq = jnp.power(ROPE_THETA, -jnp.arange(0, MLA_ROPE, 2, dtype=F32) / MLA_ROPE)
    inv = jnp.tile(inv_freq, LANES // (MLA_ROPE // 2)).reshape(1, LANES)
    pos = jnp.broadcast_to(positions.astype(F32).reshape(t, 1), (t, LANES))
    tm = min(1024, t)
    return pl.pallas_call(
        _rope_body,
        grid=(t // tm,),
        in_specs=[pl.BlockSpec((tm, LANES), lambda i: (i, 0)), pl.BlockSpec((1, LANES), lambda i: (0, 0))],
        out_specs=pl.BlockSpec((tm, LANES), lambda i: (i, 0)),
        out_shape=jax.ShapeDtypeStruct((t, LANES), F32),
        compiler_params=_cparams(("parallel",), 0),
        name="rope_table",
    )(pos, inv)


DN_HALO = SUBLANES


def _dn_body(alog_ref, dtb_ref, q_ref, k_ref, v_ref, z_ref, ab_ref, wq_ref, wk_ref, wv_ref, ng_ref, o_ref,
             state_ref, halo_ref):
    sblk = q_ref.shape[1]
    hd = DN_HEAD_DIM

    @pl.when(pl.program_id(1) == 0)
    def _():
        state_ref[...] = jnp.zeros_like(state_ref)
        halo_ref[...] = jnp.zeros_like(halo_ref)

    lane = lax.broadcasted_iota(jnp.int32, (1, LANES), 1)
    row = lax.broadcasted_iota(jnp.int32, (CHUNK, CHUNK), 0)
    col = lax.broadcasted_iota(jnp.int32, (CHUNK, CHUNK), 1)
    tril = (row >= col).astype(F32)
    row2 = lax.broadcasted_iota(jnp.int32, (2 * CHUNK, 2 * CHUNK), 0)
    col2i = lax.broadcasted_iota(jnp.int32, (2 * CHUNK, 2 * CHUNK), 1)
    same_head = (row2 // CHUNK) == (col2i // CHUNK)
    incl2 = same_head & (row2 >= col2i)
    strict2 = same_head & (row2 > col2i)
    a_scale = jnp.where(lane < DN_HEADS, -jnp.exp(alog_ref[...]), 0.0)
    dt_bias = dtb_ref[...]
    ng = ng_ref[...]
    seq_refs = (q_ref, k_ref, v_ref)
    w_refs = (wq_ref, wk_ref, wv_ref)

    def conv_silu(which, sl, c, c0):
        ref = seq_refs[which]
        cur = ref[0, pl.ds(c0, CHUNK), sl]
        p0 = pl.multiple_of(jnp.maximum(c0 - DN_HALO, 0), DN_HALO)
        prev = jnp.where(c > 0, ref[0, pl.ds(p0, DN_HALO), sl], halo_ref[which, :, sl])
        xx = jnp.concatenate([prev, cur], axis=0)
        w = w_refs[which][:, sl]
        first = DN_HALO - (DN_CONV - 1)
        y = w[0:1] * xx[first:first + CHUNK]
        for j in range(1, DN_CONV):
            y = y + w[j:j + 1] * xx[first + j:first + j + CHUNK]
        return _silu(y)

    heads = range(DN_HEADS)
    pairs = range(DN_HEADS // 2)
    unroll = 2
    items = [(ci, pr) for ci in range(unroll) for pr in pairs]
    sls = [slice(h * hd, (h + 1) * hd) for h in heads]
    top, bot = slice(0, CHUNK), slice(CHUNK, 2 * CHUNK)
    half = lambda h: top if h % 2 == 0 else bot

    def col2(a, off, pr):
        return jnp.concatenate([a[:, off + 2 * pr:off + 2 * pr + 1], a[:, off + 2 * pr + 1:off + 2 * pr + 2]], axis=0)

    def gate_terms(c0):
        ab = ab_ref[0, pl.ds(c0, CHUNK), :]
        xa = ab + dt_bias
        g_all = a_scale * (jnp.maximum(xa, 0.0) + jnp.log1p(jnp.exp(-jnp.abs(xa))))
        beta_all = jax.nn.sigmoid(ab)
        gc_all = jnp.dot(tril, g_all, precision=lax.Precision.HIGHEST, preferred_element_type=F32)
        g_last = gc_all[CHUNK - 1:CHUNK, :]
        gc_t = jnp.concatenate([gc_all, gc_all], axis=0).T
        return dict(beta=beta_all, gc=gc_all, gc_t=gc_t, eg=jnp.exp(gc_all), ed=jnp.exp(g_last - gc_all),
                    egl=jnp.exp(g_last))

    def chunk_group_step(it, carry):
        cidx = [it * unroll + ci for ci in range(unroll)]
        c0s = [pl.multiple_of(c * CHUNK, CHUNK) for c in cidx]
        gt = [gate_terms(c0) for c0 in c0s]

        def stacked(fn, ci, pr):
            return jnp.concatenate([fn(ci, 2 * pr), fn(ci, 2 * pr + 1)], axis=0)

        z2 = {(ci, pr): stacked(lambda ci, h: z_ref[0, pl.ds(c0s[ci], CHUNK), sls[h]], ci, pr) for ci, pr in items}
        q2 = {(ci, pr): stacked(lambda ci, h: conv_silu(0, sls[h], cidx[ci], c0s[ci]), ci, pr) for ci, pr in items}
        k2 = {(ci, pr): stacked(lambda ci, h: conv_silu(1, sls[h], cidx[ci], c0s[ci]), ci, pr) for ci, pr in items}
        v2 = {(ci, pr): stacked(lambda ci, h: conv_silu(2, sls[h], cidx[ci], c0s[ci]), ci, pr) for ci, pr in items}
        q2 = {i: x * lax.rsqrt(jnp.sum(x * x, axis=-1, keepdims=True) + 1e-6) * (hd ** -0.5) for i, x in q2.items()}
        k2 = {i: x * lax.rsqrt(jnp.sum(x * x, axis=-1, keepdims=True) + 1e-6) for i, x in k2.items()}
        beta2 = {(ci, pr): col2(gt[ci]["beta"], DN_HEADS, pr) for ci, pr in items}
        eg2 = {(ci, pr): col2(gt[ci]["eg"], 0, pr) for ci, pr in items}
        g_row2 = {(ci, pr): jnp.where(lane < CHUNK, gt[ci]["gc_t"][2 * pr:2 * pr + 1], gt[ci]["gc_t"][2 * pr + 1:2 * pr + 2])
                  for ci, pr in items}
        decay2 = {(ci, pr): jnp.exp(jnp.where(incl2, col2(gt[ci]["gc"], 0, pr) - g_row2[ci, pr], -jnp.inf))
                  for ci, pr in items}
        kb2 = {i: k2[i] * beta2[i] for i in items}
        km2 = {i: k2[i].astype(BF16) for i in items}
        kk2 = {i: _dot_nt(jnp.concatenate([kb2[i], q2[i]], axis=0).astype(BF16), km2[i]) for i in items}
        qk2 = {i: (kk2[i][2 * CHUNK:] * decay2[i]).astype(BF16) for i in items}
        p2 = {i: jnp.where(strict2, -(kk2[i][:2 * CHUNK] * decay2[i]), 0.0) for i in items}
        sol2 = {i: jnp.concatenate([kb2[i] * eg2[i], v2[i] * beta2[i]], axis=1) for i in items}
        sol2 = {i: sol2[i] + _dot(p2[i].astype(BF16), sol2[i].astype(BF16)) for i in items}
        for _ in range(5):
            pb = {i: p2[i].astype(BF16) for i in items}
            p2 = {i: _dot(pb[i], pb[i]) for i in items}
            sol2 = {i: sol2[i] + _dot(p2[i].astype(BF16), sol2[i].astype(BF16)) for i in items}
        wq2 = {(ci, h): jnp.concatenate([sol2[ci, h // 2][half(h), :hd], (q2[ci, h // 2] * eg2[ci, h // 2])[half(h)]],
                                        axis=0).astype(BF16) for ci in range(unroll) for h in heads}
        kd2 = {(ci, pr): (k2[ci, pr] * col2(gt[ci]["ed"], 0, pr)).astype(BF16) for ci, pr in items}
        states = [state_ref[h] for h in heads]
        for ci in range(unroll):
            ws = [_dot(wq2[ci, h], states[h].astype(BF16)) for h in heads]
            vn2 = [(sol2[ci, pr][:, hd:] - jnp.concatenate([ws[2 * pr][:CHUNK], ws[2 * pr + 1][:CHUNK]], axis=0)).astype(BF16)
                   for pr in pairs]
            states = [states[h] * gt[ci]["egl"][:, h:h + 1] + _dot_tn(kd2[ci, h // 2][half(h)], vn2[h // 2][half(h)])
                      for h in heads]
            o2 = [jnp.concatenate([ws[2 * pr][CHUNK:], ws[2 * pr + 1][CHUNK:]], axis=0) + _dot(qk2[ci, pr], vn2[pr])
                  for pr in pairs]
            out2 = [(_rms(o2[pr]) * ng * _silu(z2[ci, pr])).astype(o_ref.dtype) for pr in pairs]
            for h in heads:
                o_ref[0, pl.ds(c0s[ci], CHUNK), sls[h]] = out2[h // 2][half(h)]
        for h in heads:
            state_ref[h] = states[h]
        return carry

    lax.fori_loop(0, sblk // (CHUNK * unroll), chunk_group_step, 0)
    for which in range(3):
        halo_ref[which] = seq_refs[which][0, sblk - DN_HALO:sblk, :]


def _deltanet(proj3, conv_w, a_log, dt_bias, norm_g, *, sblk=512):
    b, s, _ = proj3.shape
    sblk = min(sblk, s)
    wd = DN_WIDTH
    pad = lambda vec: jnp.concatenate([vec, jnp.zeros((LANES - DN_HEADS,), F32)]).reshape(1, LANES)
    seq_spec = lambda blk: pl.BlockSpec((1, sblk, wd), lambda i, j: (i, j, blk))
    w_spec = lambda blk: pl.BlockSpec((DN_CONV, wd), lambda i, j: (0, blk))
    vec_spec = pl.BlockSpec((1, LANES), lambda i, j: (0, 0))
    vmem = 2 * 4 * sblk * wd * 4 + 2 * sblk * wd * 2 + 2 * sblk * LANES * 4 + 8 * 1024 * 1024
    return pl.pallas_call(
        _dn_body,
        grid=(b, s // sblk),
        in_specs=[vec_spec, vec_spec,
                  seq_spec(OFF_QKV // wd), seq_spec(OFF_QKV // wd + 1), seq_spec(OFF_QKV // wd + 2), seq_spec(OFF_Z // wd),
                  pl.BlockSpec((1, sblk, LANES), lambda i, j: (i, j, OFF_AB // LANES)),
                  w_spec(0), w_spec(1), w_spec(2), vec_spec],
        out_specs=pl.BlockSpec((1, sblk, wd), lambda i, j: (i, j, 0)),
        out_shape=jax.ShapeDtypeStruct((b, s, wd), BF16),
        scratch_shapes=[pltpu.VMEM((DN_HEADS, DN_HEAD_DIM, DN_HEAD_DIM), F32), pltpu.VMEM((3, DN_HALO, wd), F32)],
        compiler_params=_cparams(("parallel", "arbitrary"), vmem),
        name="deltanet",
    )(pad(a_log), pad(dt_bias), proj3, proj3, proj3, proj3, proj3, conv_w, conv_w, conv_w, norm_g.reshape(1, -1))


def _gelu(x):
    return 0.5 * x * (1.0 + lax.erf(x * (0.5 ** 0.5)))


def _sg_body(u_ref, v_ref, lng_ref, lnb_ref, w_ref, bt_ref, o_ref):
    u = _gelu(u_ref[0])
    v = _gelu(v_ref[0])
    mu = jnp.mean(v, axis=-1, keepdims=True)
    vc = v - mu
    var = jnp.mean(vc * vc, axis=-1, keepdims=True)
    vn = (vc * lax.rsqrt(var + 1e-5) * lng_ref[...] + lnb_ref[...]).astype(BF16)
    row = lax.broadcasted_iota(jnp.int32, (SG_BLOCK, SG_BLOCK), 0)
    col = lax.broadcasted_iota(jnp.int32, (SG_BLOCK, SG_BLOCK), 1)
    mask = (col // CHUNK) <= (row // CHUNK)
    for g in range(SG_GROUPS):
        sl = slice(g * SG_GROUP_DIM, (g + 1) * SG_GROUP_DIM)
        wg = jnp.where(mask, w_ref[g], 0.0).astype(BF16)
        mixed = _dot(wg, vn[:, sl]) + bt_ref[:, g:g + 1]
        o_ref[0, :, sl] = (u[:, sl] * mixed).astype(o_ref.dtype)


def _gmlp(proj3, ln_g, ln_b, sg_w, sg_b):
    b, s, _ = proj3.shape
    ub = OFF_SG // SG_WIDTH
    vmem = 2 * 2 * SG_BLOCK * SG_WIDTH * 4 + 2 * SG_GROUPS * SG_BLOCK * SG_BLOCK * 4 + 8 * SG_BLOCK * SG_WIDTH * 4
    return pl.pallas_call(
        _sg_body,
        grid=(b, s // SG_BLOCK),
        in_specs=[
            pl.BlockSpec((1, SG_BLOCK, SG_WIDTH), lambda i, n: (i, n, ub)),
            pl.BlockSpec((1, SG_BLOCK, SG_WIDTH), lambda i, n: (i, n, ub + 1)),
            pl.BlockSpec((1, SG_WIDTH), lambda i, n: (0, 0)),
            pl.BlockSpec((1, SG_WIDTH), lambda i, n: (0, 0)),
            pl.BlockSpec((SG_GROUPS, SG_BLOCK, SG_BLOCK), lambda i, n: (0, 0, 0)),
            pl.BlockSpec((SG_BLOCK, SG_GROUPS), lambda i, n: (0, 0)),
        ],
        out_specs=pl.BlockSpec((1, SG_BLOCK, SG_WIDTH), lambda i, n: (i, n, 0)),
        out_shape=jax.ShapeDtypeStruct((b, s, SG_WIDTH), BF16),
        compiler_params=_cparams(("parallel", "parallel"), vmem),
        name="gmlp",
    )(proj3, proj3, ln_g.reshape(1, SG_WIDTH), ln_b.reshape(1, SG_WIDTH), sg_w, sg_b.T)


def _rotate(block, cs):
    t = block * cs
    return t + pltpu.roll(t, MLA_ROPE, 1)


def _mla_proj_body(cq_ref, ckv_ref, kr_ref, cs_ref, gq_ref, gkv_ref, wq_ref, wk_ref, wvt_ref, q_out, k_out, vt_out):
    cs = cs_ref[0]
    scale = (MLA_NOPE + MLA_ROPE) ** -0.5
    cqn = (_rms(cq_ref[0]) * gq_ref[...]).astype(BF16)
    q = _dot(cqn, wq_ref[...])
    lane = lax.broadcasted_iota(jnp.int32, cs.shape, 1)
    k_rope = jnp.where(lane < MLA_ROPE, _rotate(kr_ref[0], cs), 0.0).astype(k_out.dtype)
    ckvn = (_rms(ckv_ref[0]) * gkv_ref[...]).astype(BF16)
    k = _dot(ckvn, wk_ref[...])
    for h in range(MLA_HEADS):
        base = h * MLA_QW
        q_out[0, :, base:base + MLA_NOPE] = (q[:, base:base + MLA_NOPE] * scale).astype(q_out.dtype)
        q_out[0, :, base + MLA_NOPE:base + MLA_QW] = (_rotate(q[:, base + MLA_NOPE:base + MLA_QW], cs) * scale).astype(q_out.dtype)
        k_out[0, :, base:base + MLA_NOPE] = k[:, h * MLA_NOPE:(h + 1) * MLA_NOPE].astype(k_out.dtype)
        k_out[0, :, base + MLA_NOPE:base + MLA_QW] = k_rope
    vt_out[0] = _dot_nt(wvt_ref[...], ckvn).astype(vt_out.dtype)


def _mla_proj(proj3, cs3, cq_g, ckv_g, w_uq, w_uk, w_uvt, *, tm=512):
    b, s, _ = proj3.shape
    tm = min(tm, s)
    nq = w_uq.shape[1]
    nk = MLA_HEADS * MLA_QW
    nv = w_uvt.shape[0]
    vmem = (2 * tm * (MLA_Q_RANK + MLA_KV_RANK + 2 * LANES) * 4 + 2 * MLA_Q_RANK * (nq + w_uk.shape[1] + nv) * 2
            + 2 * tm * (nq + nk + nv) * 2 + 3 * tm * (nq + nk + nv) * 4)
    const = lambda shape: pl.BlockSpec(shape, lambda bi, i: (0, 0))
    return pl.pallas_call(
        _mla_proj_body,
        grid=(b, s // tm),
        in_specs=[
            pl.BlockSpec((1, tm, MLA_Q_RANK), lambda bi, i: (bi, i, OFF_CQ // MLA_Q_RANK)),
            pl.BlockSpec((1, tm, MLA_KV_RANK), lambda bi, i: (bi, i, OFF_CKV // MLA_KV_RANK)),
            pl.BlockSpec((1, tm, LANES), lambda bi, i: (bi, i, OFF_KR // LANES)),
            pl.BlockSpec((1, tm, LANES), lambda bi, i: (bi, i, 0)),
            const((1, MLA_Q_RANK)), const((1, MLA_KV_RANK)),
            const(w_uq.shape), const(w_uk.shape), const(w_uvt.shape),
        ],
        out_specs=[
            pl.BlockSpec((1, tm, nq), lambda bi, i: (bi, i, 0)),
            pl.BlockSpec((1, tm, nk), lambda bi, i: (bi, i, 0)),
            pl.BlockSpec((1, nv, tm), lambda bi, i: (bi, 0, i)),
        ],
        out_shape=[jax.ShapeDtypeStruct((b, s, nq), BF16), jax.ShapeDtypeStruct((b, s, nk), BF16),
                   jax.ShapeDtypeStruct((b, nv, s), BF16)],
        compiler_params=_cparams(("parallel", "parallel"), vmem),
        name="mla_proj",
    )(proj3, proj3, proj3, cs3, cq_g.reshape(1, -1), ckv_g.reshape(1, -1), w_uq, w_uk, w_uvt)


ATTN_HEADS_PER_STEP = 8


def _attn_body(q_ref, k_ref, vt_ref, o_ref, *, tq, nh):
    i = pl.program_id(2)
    diag_mask = (lax.broadcasted_iota(jnp.int32, (tq, tq), 0) // CHUNK
                 <= lax.broadcasted_iota(jnp.int32, (tq, tq), 1) // CHUNK)

    def kv_step(j, carry, masked):
        k0 = pl.multiple_of(j * tq, tq)
        heads = range(nh)
        st = [_dot_nt(k_ref[0, pl.ds(k0, tq), h * MLA_QW:(h + 1) * MLA_QW], q_ref[0, :, h * MLA_QW:(h + 1) * MLA_QW])
              for h in heads]
        if masked:
            st = [jnp.where(diag_mask, x, -1e30) for x in st]
        m_new = [jnp.maximum(carry[h][0], jnp.max(st[h], axis=0, keepdims=True)) for h in heads]
        alpha = [jnp.exp(carry[h][0] - m_new[h]) for h in heads]
        pt = [jnp.exp(st[h] - m_new[h]) for h in heads]
        l = [alpha[h] * carry[h][1] + jnp.sum(pt[h], axis=0, keepdims=True) for h in heads]
        pv = [_dot(vt_ref[0, h * MLA_V:(h + 1) * MLA_V, pl.ds(k0, tq)], pt[h].astype(BF16)) for h in heads]
        return tuple((m_new[h], l[h], alpha[h] * carry[h][2] + pv[h]) for h in heads)

    init = tuple((jnp.full((1, tq), -jnp.inf, F32), jnp.zeros((1, tq), F32), jnp.zeros((MLA_V, tq), F32))
                 for _ in range(nh))
    carry = lax.fori_loop(0, i, functools.partial(kv_step, masked=False), init)
    carry = kv_step(i, carry, masked=True)
    for h in range(nh):
        _, l, acc = carry[h]
        o_ref[0, :, h * MLA_V:(h + 1) * MLA_V] = (acc / l).T.astype(o_ref.dtype)


def _attention(q3, k3, vt3, *, tq=256, nh=ATTN_HEADS_PER_STEP):
    b, s, _ = q3.shape
    tq = min(tq, s)
    vmem = (2 * tq * nh * MLA_QW * 2 + 2 * s * nh * MLA_QW * 2 + 2 * s * nh * MLA_V * 2 + 2 * tq * nh * MLA_V * 2
            + nh * 8 * tq * tq * 4)
    return pl.pallas_call(
        functools.partial(_attn_body, tq=tq, nh=nh),
        grid=(b, MLA_HEADS // nh, s // tq),
        in_specs=[
            pl.BlockSpec((1, tq, nh * MLA_QW), lambda bi, hg, i: (bi, i, hg)),
            pl.BlockSpec((1, s, nh * MLA_QW), lambda bi, hg, i: (bi, 0, hg)),
            pl.BlockSpec((1, nh * MLA_V, s), lambda bi, hg, i: (bi, hg, 0)),
        ],
        out_specs=pl.BlockSpec((1, tq, nh * MLA_V), lambda bi, hg, i: (bi, i, hg)),
        out_shape=jax.ShapeDtypeStruct((b, s, MLA_HEADS * MLA_V), BF16),
        compiler_params=_cparams(("parallel", "parallel", "arbitrary"), vmem),
        name="mla_attention",
    )(q3, k3, vt3)


def _merge_body(x_ref, oa_ref, ob_ref, oc_ref, ga_ref, gb_ref, gc_ref, wa_ref, wb_ref, wc_ref, wo_ref, g_ref,
                o_ref, *, nc):
    c = pl.program_id(1)

    @pl.when(c == 0)
    def _():
        o_ref[...] = jnp.zeros_like(o_ref)

    merged = jax.nn.sigmoid(ga_ref[...]) * _dot(oa_ref[...], wa_ref[...])
    merged = merged + jax.nn.sigmoid(gb_ref[...]) * _dot(ob_ref[...], wb_ref[...])
    merged = merged + jax.nn.sigmoid(gc_ref[...]) * _dot(oc_ref[...], wc_ref[...])
    o_ref[...] += _dot(merged.astype(BF16), wo_ref[...])

    @pl.when(c == nc - 1)
    def _():
        o_ref[...] = x_ref[...] + _rms(o_ref[...]) * g_ref[...]


def _merge(x, oa, ob, oc, proj, w_branch, w_out, g, *, tm=512, tc=512):
    t, d = x.shape
    tm = min(tm, t)
    nc = d // tc
    bw = oa.shape[1]
    gate_blk = OFF_GATE // tc
    vmem = (2 * tm * d * 4 + 2 * tm * d * 4 + 2 * 3 * tm * bw * 2 + 2 * 3 * tm * tc * 4 + 2 * 3 * bw * tc * 2
            + 2 * tc * d * 2 + 6 * tm * tc * 4 + tm * d * 4)
    o_spec = pl.BlockSpec((tm, bw), lambda i, c: (i, 0))
    gate_spec = lambda n: pl.BlockSpec((tm, tc), lambda i, c: (i, gate_blk + n * nc + c))
    wb_spec = lambda n: pl.BlockSpec((bw, tc), lambda i, c: (n, c))
    return pl.pallas_call(
        functools.partial(_merge_body, nc=nc),
        grid=(t // tm, nc),
        in_specs=[pl.BlockSpec((tm, d), lambda i, c: (i, 0)), o_spec, o_spec, o_spec,
                  gate_spec(0), gate_spec(1), gate_spec(2), wb_spec(0), wb_spec(1), wb_spec(2),
                  pl.BlockSpec((tc, d), lambda i, c: (c, 0)),
                  pl.BlockSpec((1, d), lambda i, c: (0, 0))],
        out_specs=pl.BlockSpec((tm, d), lambda i, c: (i, 0)),
        out_shape=jax.ShapeDtypeStruct((t, d), F32),
        compiler_params=_cparams(("parallel", "arbitrary"), vmem),
        name="merge",
    )(x, oa, ob, oc, proj, proj, proj, w_branch, w_branch, w_branch, w_out, g.reshape(1, d))


def kernel(x, positions, norm_g, ffn_w_gate, ffn_w_up, ffn_w_down, w_in, dn_conv_w, dn_a_log, dn_dt_bias, dn_norm_g, sg_ln_g, sg_ln_b, sg_w, sg_b, mla_cq_norm_g, mla_ckv_norm_g, mla_w_uq, mla_w_ukv, w_branch, w_out):
    b, s, d = x.shape
    t = b * s
    xf = x.reshape(t, d)
    cs = _rope_table(positions)
    w_branch2 = w_branch.reshape(DEPTH, N_BRANCH * BRANCH_WIDTH, d)
    for l in range(DEPTH):
        ng = norm_g[l]
        xf = _ffn(xf, ng[0], _cast_bf16(ffn_w_gate, (l, 0)), _cast_bf16(ffn_w_up, (l, 0)),
                  _cast_bf16(ffn_w_down, (l, 0)), ng[1])
        proj = _norm_proj(xf, ng[2], _relayout_w_in(w_in, l))
        proj3 = proj.reshape(b, s, PROJ_COLS)
        o_a = _deltanet(proj3, dn_conv_w[l], dn_a_log[l], dn_dt_bias[l], dn_norm_g[l])
        o_b = _gmlp(proj3, sg_ln_g[l], sg_ln_b[l], sg_w[l], sg_b[l])
        w_uk, w_uvt = _split_w_ukv(mla_w_ukv[l])
        q, k, vt = _mla_proj(proj3, cs.reshape(b, s, LANES), mla_cq_norm_g[l], mla_ckv_norm_g[l],
                             _relayout_w_uq(mla_w_uq[l]), w_uk, w_uvt)
        o_c = _attention(q, k, vt)
        xf = _merge(xf, o_a.reshape(t, -1), o_b.reshape(t, -1), o_c.reshape(t, -1), proj,
                    _cast_bf16(w_branch2, (l,)), _cast_bf16(w_out, (l,)), ng[3])
        xf = _ffn(xf, ng[4], _cast_bf16(ffn_w_gate, (l, 1)), _cast_bf16(ffn_w_up, (l, 1)),
                  _cast_bf16(ffn_w_down, (l, 1)), ng[5])
    return xf.reshape(b, s, d)
```

```python
import functools

import jax
import jax.numpy as jnp
from jax import lax
from jax.experimental import pallas as pl
from jax.experimental.pallas import tpu as pltpu

F32 = jnp.float32
BF16 = jnp.bfloat16

D_MODEL = 2048
DEPTH = 2
CHUNK = 64
NORM_EPS = 1e-6
DN_HEADS = 8
DN_HEAD_DIM = 128
DN_WIDTH = DN_HEADS * DN_HEAD_DIM
DN_CONV = 4
SG_GROUPS = 8
SG_GROUP_DIM = 128
SG_WIDTH = SG_GROUPS * SG_GROUP_DIM
SG_BLOCK = 128
MLA_HEADS = 8
MLA_Q_RANK = 512
MLA_KV_RANK = 512
MLA_NOPE = 128
MLA_ROPE = 64
MLA_V = 128
ROPE_THETA = 10000.0
N_BRANCH = 3
BRANCH_WIDTH = 1024
D_FF = 5632

LANES = 128
SUBLANES = 8
V7X_VMEM_BYTES = 64 * 1024 * 1024
VMEM_CAP = V7X_VMEM_BYTES - 6 * 1024 * 1024

OFF_QKV = 0
OFF_Z = OFF_QKV + 3 * DN_WIDTH
OFF_SG = OFF_Z + DN_WIDTH
OFF_GATE = OFF_SG + 2 * SG_WIDTH
OFF_CQ = OFF_GATE + N_BRANCH * D_MODEL
OFF_CKV = OFF_CQ + MLA_Q_RANK
OFF_AB = OFF_CKV + MLA_KV_RANK
OFF_KR = OFF_AB + LANES
PROJ_TILE = 512
PROJ_COLS = -(-(OFF_KR + LANES) // PROJ_TILE) * PROJ_TILE
PROJ_TAIL = PROJ_COLS - OFF_AB
MLA_QW = 2 * LANES

SRC_AB = 4 * DN_WIDTH
SRC_SG = SRC_AB + 2 * DN_HEADS
SRC_CQ = SRC_SG + 2 * SG_WIDTH
SRC_CKV = SRC_CQ + MLA_Q_RANK
SRC_KR = SRC_CKV + MLA_KV_RANK
SRC_GATE = SRC_KR + MLA_ROPE
IN_COLS = SRC_GATE + N_BRANCH * D_MODEL


def _cparams(semantics, vmem_bytes):
    return pltpu.CompilerParams(dimension_semantics=semantics,
                                vmem_limit_bytes=int(min(max(vmem_bytes, 16 * 1024 * 1024), VMEM_CAP)))


def _dot(a, b):
    return jnp.dot(a, b, preferred_element_type=F32)


def _dot_nt(a, b):
    return lax.dot_general(a, b, (((1,), (1,)), ((), ())), preferred_element_type=F32)


def _dot_tn(a, b):
    return lax.dot_general(a, b, (((0,), (0,)), ((), ())), preferred_element_type=F32)


def _rms(x, eps=NORM_EPS):
    return x * lax.rsqrt(jnp.mean(x * x, axis=-1, keepdims=True) + eps)


def _silu(x):
    return x * jax.nn.sigmoid(x)


def _cast_body(w_ref, o_ref):
    o_ref[...] = w_ref[...].astype(o_ref.dtype)


def _cast_bf16(w, *, block_bytes=4 * 1024 * 1024):
    r, c = w.shape[-2:]
    w = w.reshape(-1, r, c)
    n = w.shape[0]
    rb = r
    while rb * c * 4 > block_bytes and rb % (4 * SUBLANES) == 0:
        rb //= 2
    return pl.pallas_call(
        _cast_body,
        grid=(n, r // rb),
        in_specs=[pl.BlockSpec((1, rb, c), lambda m, i: (m, i, 0))],
        out_specs=pl.BlockSpec((1, rb, c), lambda m, i: (m, i, 0)),
        out_shape=jax.ShapeDtypeStruct((n, r, c), BF16),
        compiler_params=_cparams(("parallel", "parallel"), 3 * 2 * rb * c * 4),
        name="cast_bf16",
    )(w)


_WIDE_SEGMENTS = ((OFF_QKV, 0, SRC_AB), (OFF_SG, SRC_SG, 2 * SG_WIDTH), (OFF_GATE, SRC_GATE, N_BRANCH * D_MODEL),
                  (OFF_CQ, SRC_CQ, MLA_Q_RANK), (OFF_CKV, SRC_CKV, MLA_KV_RANK))


def _relayout_body(wt_ref, tail_ref, o_ref, *, n_wide):
    i = pl.program_id(0)

    @pl.when(i < n_wide)
    def _():
        o_ref[...] = wt_ref[0].T.astype(BF16)

    @pl.when(i >= n_wide)
    def _():
        o_ref[...] = tail_ref[...].astype(BF16)


def _swap_halves(w):
    half = w.shape[-1] // 2
    return jnp.concatenate([-w[..., half:], w[..., :half]], axis=-1)


def _relayout_w_in(w_in, l):
    d = w_in.shape[1]
    tb = PROJ_TAIL
    kr = w_in[l, :, SRC_KR:SRC_GATE]
    tail = jnp.concatenate([w_in[l, :, SRC_AB:SRC_SG], jnp.zeros((d, LANES - 2 * DN_HEADS), F32), kr, _swap_halves(kr),
                            jnp.zeros((d, PROJ_TAIL - 2 * LANES), F32)], axis=1)
    n_wide = OFF_AB // tb

    def src_start(i):
        dst = i * tb
        start = 0
        for d0, s0, width in _WIDE_SEGMENTS:
            start = jnp.where((dst >= d0) & (dst < d0 + width), s0 + dst - d0, start)
        return pl.multiple_of(start, 2 * SUBLANES)

    return pl.pallas_call(
        functools.partial(_relayout_body, n_wide=n_wide),
        grid=(PROJ_COLS // tb,),
        in_specs=[pl.BlockSpec((pl.Element(1), pl.Element(tb), pl.Element(d)), lambda i: (l, src_start(i), 0)),
                  pl.BlockSpec((d, tb), lambda i: (0, 0))],
        out_specs=pl.BlockSpec((d, tb), lambda i: (0, i)),
        out_shape=jax.ShapeDtypeStruct((d, PROJ_COLS), BF16),
        compiler_params=_cparams(("parallel",), 2 * tb * d * 4 + 2 * tb * d * 4 + 2 * tb * d * 2 + 2 * tb * d * 4),
        name="relayout_w_in",
    )(jnp.swapaxes(w_in, 1, 2), tail)


def _relayout_w_uq(w):
    r = w.shape[0]
    w = w.reshape(r, MLA_HEADS, MLA_NOPE + MLA_ROPE)
    rope = w[..., MLA_NOPE:]
    return jnp.concatenate([w[..., :MLA_NOPE], rope, _swap_halves(rope)], axis=-1).reshape(r, MLA_HEADS * MLA_QW).astype(BF16)


def _split_w_ukv(w):
    r = w.shape[0]
    w = w.reshape(r, MLA_HEADS, MLA_NOPE + MLA_V)
    w_uk = w[..., :MLA_NOPE].reshape(r, MLA_HEADS * MLA_NOPE).astype(BF16)
    w_uvt = w[..., MLA_NOPE:].reshape(r, MLA_HEADS * MLA_V).T.astype(BF16)
    return w_uk, w_uvt


def _ffn_body(x_ref, gpre_ref, wg_ref, wu_ref, wd_ref, gpost_ref, o_ref, xn_ref, *, nf):
    f = pl.program_id(1)

    @pl.when(f == 0)
    def _():
        xn_ref[...] = (_rms(x_ref[...]) * gpre_ref[...]).astype(BF16)
        o_ref[...] = jnp.zeros_like(o_ref)

    xn = xn_ref[...]
    h = _dot(xn, wg_ref[...])
    u = _dot(xn, wu_ref[...])
    a = (_silu(h) * u).astype(BF16)
    o_ref[...] += _dot(a, wd_ref[...])

    @pl.when(f == nf - 1)
    def _():
        o_ref[...] = x_ref[...] + 0.5 * (_rms(o_ref[...]) * gpost_ref[...])


def _ffn(x, g_pre, w_gate, w_up, w_down, wi, g_post, *, tm=512, tf=512):
    t, d = x.shape
    tm = min(tm, t)
    dff = w_gate.shape[-1]
    nf = dff // tf
    vmem = (2 * tm * d * 4 + 2 * tm * d * 4 + tm * d * 2 + 2 * 3 * d * tf * 2 + 5 * tm * tf * 4 + tm * d * 4)
    return pl.pallas_call(
        functools.partial(_ffn_body, nf=nf),
        grid=(t // tm, nf),
        in_specs=[
            pl.BlockSpec((tm, d), lambda i, f: (i, 0)),
            pl.BlockSpec((1, d), lambda i, f: (0, 0)),
            pl.BlockSpec((None, d, tf), lambda i, f: (wi, 0, f)),
            pl.BlockSpec((None, d, tf), lambda i, f: (wi, 0, f)),
            pl.BlockSpec((None, tf, d), lambda i, f: (wi, f, 0)),
            pl.BlockSpec((1, d), lambda i, f: (0, 0)),
        ],
        out_specs=pl.BlockSpec((tm, d), lambda i, f: (i, 0)),
        out_shape=jax.ShapeDtypeStruct((t, d), F32),
        scratch_shapes=[pltpu.VMEM((tm, d), BF16)],
        compiler_params=_cparams(("parallel", "arbitrary"), vmem),
        name="ffn",
    )(x, g_pre.reshape(1, d), w_gate, w_up, w_down, g_post.reshape(1, d))


def _proj_body(x_ref, g_ref, w_ref, o_ref, xn_ref):
    @pl.when(pl.program_id(1) == 0)
    def _():
        xn_ref[...] = (_rms(x_ref[...]) * g_ref[...]).astype(BF16)

    o_ref[...] = _dot(xn_ref[...], w_ref[...])


def _norm_proj(x, g, w, *, tm=1024, tn=3 * PROJ_TILE):
    t, d = x.shape
    n = w.shape[1]
    tm = min(tm, t)
    vmem = tm * d * 4 + tm * d * 2 + 2 * d * tn * 2 + 3 * tm * tn * 4 + tm * d * 4
    return pl.pallas_call(
        _proj_body,
        grid=(t // tm, n // tn),
        in_specs=[
            pl.BlockSpec((tm, d), lambda i, j: (i, 0), pipeline_mode=pl.Buffered(1)),
            pl.BlockSpec((1, d), lambda i, j: (0, 0)),
            pl.BlockSpec((d, tn), lambda i, j: (0, j)),
        ],
        out_specs=pl.BlockSpec((tm, tn), lambda i, j: (i, j)),
        out_shape=jax.ShapeDtypeStruct((t, n), F32),
        scratch_shapes=[pltpu.VMEM((tm, d), BF16)],
        compiler_params=_cparams(("parallel", "arbitrary"), vmem),
        name="norm_proj",
    )(x, g.reshape(1, d), w)


def _rope_body(pos_ref, inv_ref, o_ref):
    ang = pos_ref[...] * inv_ref[...]
    lane = lax.broadcasted_iota(jnp.int32, ang.shape, 1)
    o_ref[...] = jnp.where(lane < 2 * (MLA_ROPE // 2), jnp.cos(ang), jnp.sin(ang))


def _rope_table(positions):
    b, s = positions.shape
    t = b * s
    inv_freq = jnp.power(ROPE_THETA, -jnp.arange(0, MLA_ROPE, 2, dtype=F32) / MLA_ROPE)
    inv = jnp.tile(inv_freq, LANES // (MLA_ROPE // 2)).reshape(1, LANES)
    pos = jnp.broadcast_to(positions.astype(F32).reshape(t, 1), (t, LANES))
    tm = min(1024, t)
    return pl.pallas_call(
        _rope_body,
        grid=(t // tm,),
        in_specs=[pl.BlockSpec((tm, LANES), lambda i: (i, 0)), pl.BlockSpec((1, LANES), lambda i: (0, 0))],
        out_specs=pl.BlockSpec((tm, LANES), lambda i: (i, 0)),
        out_shape=jax.ShapeDtypeStruct((t, LANES), F32),
        compiler_params=_cparams(("parallel",), 0),
        name="rope_table",
    )(pos, inv)


DN_HALO = SUBLANES


def _dn_body(alog_ref, dtb_ref, q_ref, k_ref, v_ref, z_ref, ab_ref, wq_ref, wk_ref, wv_ref, ng_ref, o_ref,
             state_ref, halo_ref):
    sblk = q_ref.shape[1]
    hd = DN_HEAD_DIM

    @pl.when(pl.program_id(1) == 0)
    def _():
        state_ref[...] = jnp.zeros_like(state_ref)
        halo_ref[...] = jnp.zeros_like(halo_ref)

    lane = lax.broadcasted_iota(jnp.int32, (1, LANES), 1)
    row = lax.broadcasted_iota(jnp.int32, (CHUNK, CHUNK), 0)
    col = lax.broadcasted_iota(jnp.int32, (CHUNK, CHUNK), 1)
    tril = (row >= col).astype(F32)
    row2 = lax.broadcasted_iota(jnp.int32, (2 * CHUNK, 2 * CHUNK), 0)
    col2i = lax.broadcasted_iota(jnp.int32, (2 * CHUNK, 2 * CHUNK), 1)
    same_head = (row2 // CHUNK) == (col2i // CHUNK)
    incl2 = same_head & (row2 >= col2i)
    strict2 = same_head & (row2 > col2i)
    a_scale = jnp.where(lane < DN_HEADS, -jnp.exp(alog_ref[...]), 0.0)
    dt_bias = dtb_ref[...]
    ng = ng_ref[...]
    seq_refs = (q_ref, k_ref, v_ref)
    w_refs = (wq_ref, wk_ref, wv_ref)

    def conv_silu(which, sl, c, c0):
        ref = seq_refs[which]
        cur = ref[0, pl.ds(c0, CHUNK), sl]
        p0 = pl.multiple_of(jnp.maximum(c0 - DN_HALO, 0), DN_HALO)
        prev = jnp.where(c > 0, ref[0, pl.ds(p0, DN_HALO), sl], halo_ref[which, :, sl])
        xx = jnp.concatenate([prev, cur], axis=0)
        w = w_refs[which][:, sl]
        first = DN_HALO - (DN_CONV - 1)
        y = w[0:1] * xx[first:first + CHUNK]
        for j in range(1, DN_CONV):
            y = y + w[j:j + 1] * xx[first + j:first + j + CHUNK]
        return _silu(y)

    heads = range(DN_HEADS)
    pairs = range(DN_HEADS // 2)
    unroll = 4
    items = [(ci, pr) for ci in range(unroll) for pr in pairs]
    sls = [slice(h * hd, (h + 1) * hd) for h in heads]
    top, bot = slice(0, CHUNK), slice(CHUNK, 2 * CHUNK)
    half = lambda h: top if h % 2 == 0 else bot

    def col2(a, off, pr):
        return jnp.concatenate([a[:, off + 2 * pr:off + 2 * pr + 1], a[:, off + 2 * pr + 1:off + 2 * pr + 2]], axis=0)

    def gate_terms(c0):
        ab = ab_ref[0, pl.ds(c0, CHUNK), :]
        xa = ab + dt_bias
        g_all = a_scale * (jnp.maximum(xa, 0.0) + jnp.log1p(jnp.exp(-jnp.abs(xa))))
        beta_all = jax.nn.sigmoid(ab)
        gc_all = jnp.dot(tril, g_all, precision=lax.Precision.HIGHEST, preferred_element_type=F32)
        g_last = gc_all[CHUNK - 1:CHUNK, :]
        gc_t = jnp.concatenate([gc_all, gc_all], axis=0).T
        return dict(beta=beta_all, gc=gc_all, gc_t=gc_t, eg=jnp.exp(gc_all), ed=jnp.exp(g_last - gc_all),
                    egl=jnp.exp(g_last))

    def chunk_group_step(it, carry):
        cidx = [it * unroll + ci for ci in range(unroll)]
        c0s = [pl.multiple_of(c * CHUNK, CHUNK) for c in cidx]
        gt = [gate_terms(c0) for c0 in c0s]

        def stacked(fn, ci, pr):
            return jnp.concatenate([fn(ci, 2 * pr), fn(ci, 2 * pr + 1)], axis=0)

        z2 = {(ci, pr): stacked(lambda ci, h: z_ref[0, pl.ds(c0s[ci], CHUNK), sls[h]], ci, pr) for ci, pr in items}
        q2 = {(ci, pr): stacked(lambda ci, h: conv_silu(0, sls[h], cidx[ci], c0s[ci]), ci, pr) for ci, pr in items}
        k2 = {(ci, pr): stacked(lambda ci, h: conv_silu(1, sls[h], cidx[ci], c0s[ci]), ci, pr) for ci, pr in items}
        v2 = {(ci, pr): stacked(lambda ci, h: conv_silu(2, sls[h], cidx[ci], c0s[ci]), ci, pr) for ci, pr in items}
        q2 = {i: x * lax.rsqrt(jnp.sum(x * x, axis=-1, keepdims=True) + 1e-6) * (hd ** -0.5) for i, x in q2.items()}
        k2 = {i: x * lax.rsqrt(jnp.sum(x * x, axis=-1, keepdims=True) + 1e-6) for i, x in k2.items()}
        beta2 = {(ci, pr): col2(gt[ci]["beta"], DN_HEADS, pr) for ci, pr in items}
        eg2 = {(ci, pr): col2(gt[ci]["eg"], 0, pr) for ci, pr in items}
        g_row2 = {(ci, pr): jnp.where(lane < CHUNK, gt[ci]["gc_t"][2 * pr:2 * pr + 1], gt[ci]["gc_t"][2 * pr + 1:2 * pr + 2])
                  for ci, pr in items}
        decay2 = {(ci, pr): jnp.exp(jnp.where(incl2, col2(gt[ci]["gc"], 0, pr) - g_row2[ci, pr], -jnp.inf))
                  for ci, pr in items}
        kb2 = {i: k2[i] * beta2[i] for i in items}
        km2 = {i: k2[i].astype(BF16) for i in items}
        kk2 = {i: _dot_nt(jnp.concatenate([kb2[i], q2[i]], axis=0).astype(BF16), km2[i]) for i in items}
        qk2 = {i: (kk2[i][2 * CHUNK:] * decay2[i]).astype(BF16) for i in items}
        p2 = {i: jnp.where(strict2, -(kk2[i][:2 * CHUNK] * decay2[i]), 0.0) for i in items}
        sol2 = {i: jnp.concatenate([kb2[i] * eg2[i], v2[i] * beta2[i]], axis=1) for i in items}
        sol2 = {i: sol2[i] + _dot(p2[i].astype(BF16), sol2[i].astype(BF16)) for i in items}
        for _ in range(5):
            pb = {i: p2[i].astype(BF16) for i in items}
            p2 = {i: _dot(pb[i], pb[i]) for i in items}
            sol2 = {i: sol2[i] + _dot(p2[i].astype(BF16), sol2[i].astype(BF16)) for i in items}
        wq2 = {(ci, h): jnp.concatenate([sol2[ci, h // 2][half(h), :hd], (q2[ci, h // 2] * eg2[ci, h // 2])[half(h)]],
                                        axis=0).astype(BF16) for ci in range(unroll) for h in heads}
        kd2 = {(ci, pr): (k2[ci, pr] * col2(gt[ci]["ed"], 0, pr)).astype(BF16) for ci, pr in items}
        states = [state_ref[h] for h in heads]
        for ci in range(unroll):
            ws = [_dot(wq2[ci, h], states[h].astype(BF16)) for h in heads]
            vn2 = [(sol2[ci, pr][:, hd:] - jnp.concatenate([ws[2 * pr][:CHUNK], ws[2 * pr + 1][:CHUNK]], axis=0)).astype(BF16)
                   for pr in pairs]
            states = [states[h] * gt[ci]["egl"][:, h:h + 1] + _dot_tn(kd2[ci, h // 2][half(h)], vn2[h // 2][half(h)])
                      for h in heads]
            o2 = [jnp.concatenate([ws[2 * pr][CHUNK:], ws[2 * pr + 1][CHUNK:]], axis=0) + _dot(qk2[ci, pr], vn2[pr])
                  for pr in pairs]
            out2 = [(_rms(o2[pr]) * ng * _silu(z2[ci, pr])).astype(o_ref.dtype) for pr in pairs]
            for h in heads:
                o_ref[0, pl.ds(c0s[ci], CHUNK), sls[h]] = out2[h // 2][half(h)]
        for h in heads:
            state_ref[h] = states[h]
        return carry

    lax.fori_loop(0, sblk // (CHUNK * unroll), chunk_group_step, 0)
    for which in range(3):
        halo_ref[which] = seq_refs[which][0, sblk - DN_HALO:sblk, :]


def _deltanet(proj3, conv_w, a_log, dt_bias, norm_g, *, sblk=512):
    b, s, _ = proj3.shape
    sblk = min(sblk, s)
    wd = DN_WIDTH
    pad = lambda vec: jnp.concatenate([vec, jnp.zeros((LANES - DN_HEADS,), F32)]).reshape(1, LANES)
    seq_spec = lambda blk: pl.BlockSpec((1, sblk, wd), lambda i, j: (i, j, blk))
    w_spec = lambda blk: pl.BlockSpec((DN_CONV, wd), lambda i, j: (0, blk))
    vec_spec = pl.BlockSpec((1, LANES), lambda i, j: (0, 0))
    vmem = 2 * 4 * sblk * wd * 4 + 2 * sblk * wd * 2 + 2 * sblk * LANES * 4 + 8 * 1024 * 1024
    return pl.pallas_call(
        _dn_body,
        grid=(b, s // sblk),
        in_specs=[vec_spec, vec_spec,
                  seq_spec(OFF_QKV // wd), seq_spec(OFF_QKV // wd + 1), seq_spec(OFF_QKV // wd + 2), seq_spec(OFF_Z // wd),
                  pl.BlockSpec((1, sblk, LANES), lambda i, j: (i, j, OFF_AB // LANES)),
                  w_spec(0), w_spec(1), w_spec(2), vec_spec],
        out_specs=pl.BlockSpec((1, sblk, wd), lambda i, j: (i, j, 0)),
        out_shape=jax.ShapeDtypeStruct((b, s, wd), BF16),
        scratch_shapes=[pltpu.VMEM((DN_HEADS, DN_HEAD_DIM, DN_HEAD_DIM), F32), pltpu.VMEM((3, DN_HALO, wd), F32)],
        compiler_params=_cparams(("parallel", "arbitrary"), vmem),
        name="deltanet",
    )(pad(a_log), pad(dt_bias), proj3, proj3, proj3, proj3, proj3, conv_w, conv_w, conv_w, norm_g.reshape(1, -1))


def _gelu(x):
    return 0.5 * x * (1.0 + lax.erf(x * (0.5 ** 0.5)))


def _sg_body(u_ref, v_ref, lng_ref, lnb_ref, w_ref, bt_ref, o_ref):
    u = _gelu(u_ref[0])
    v = _gelu(v_ref[0])
    mu = jnp.mean(v, axis=-1, keepdims=True)
    vc = v - mu
    var = jnp.mean(vc * vc, axis=-1, keepdims=True)
    vn = (vc * lax.rsqrt(var + 1e-5) * lng_ref[...] + lnb_ref[...]).astype(BF16)
    row = lax.broadcasted_iota(jnp.int32, (SG_BLOCK, SG_BLOCK), 0)
    col = lax.broadcasted_iota(jnp.int32, (SG_BLOCK, SG_BLOCK), 1)
    mask = (col // CHUNK) <= (row // CHUNK)
    for g in range(SG_GROUPS):
        sl = slice(g * SG_GROUP_DIM, (g + 1) * SG_GROUP_DIM)
        wg = jnp.where(mask, w_ref[g], 0.0).astype(BF16)
        mixed = _dot(wg, vn[:, sl]) + bt_ref[:, g:g + 1]
        o_ref[0, :, sl] = (u[:, sl] * mixed).astype(o_ref.dtype)


def _gmlp(proj3, ln_g, ln_b, sg_w, sg_b):
    b, s, _ = proj3.shape
    ub = OFF_SG // SG_WIDTH
    vmem = 2 * 2 * SG_BLOCK * SG_WIDTH * 4 + 2 * SG_GROUPS * SG_BLOCK * SG_BLOCK * 4 + 8 * SG_BLOCK * SG_WIDTH * 4
    return pl.pallas_call(
        _sg_body,
        grid=(b, s // SG_BLOCK),
        in_specs=[
            pl.BlockSpec((1, SG_BLOCK, SG_WIDTH), lambda i, n: (i, n, ub)),
            pl.BlockSpec((1, SG_BLOCK, SG_WIDTH), lambda i, n: (i, n, ub + 1)),
            pl.BlockSpec((1, SG_WIDTH), lambda i, n: (0, 0)),
            pl.BlockSpec((1, SG_WIDTH), lambda i, n: (0, 0)),
            pl.BlockSpec((SG_GROUPS, SG_BLOCK, SG_BLOCK), lambda i, n: (0, 0, 0)),
            pl.BlockSpec((SG_BLOCK, SG_GROUPS), lambda i, n: (0, 0)),
        ],
        out_specs=pl.BlockSpec((1, SG_BLOCK, SG_WIDTH), lambda i, n: (i, n, 0)),
        out_shape=jax.ShapeDtypeStruct((b, s, SG_WIDTH), BF16),
        compiler_params=_cparams(("parallel", "parallel"), vmem),
        name="gmlp",
    )(proj3, proj3, ln_g.reshape(1, SG_WIDTH), ln_b.reshape(1, SG_WIDTH), sg_w, sg_b.T)


def _rotate(block, cs):
    t = block * cs
    return t + pltpu.roll(t, MLA_ROPE, 1)


def _mla_proj_body(cq_ref, ckv_ref, kr_ref, cs_ref, gq_ref, gkv_ref, wq_ref, wk_ref, wvt_ref, q_out, k_out, vt_out):
    cs = cs_ref[0]
    scale = (MLA_NOPE + MLA_ROPE) ** -0.5
    cqn = (_rms(cq_ref[0]) * gq_ref[...]).astype(BF16)
    q = _dot(cqn, wq_ref[...])
    lane = lax.broadcasted_iota(jnp.int32, cs.shape, 1)
    k_rope = jnp.where(lane < MLA_ROPE, _rotate(kr_ref[0], cs), 0.0).astype(k_out.dtype)
    ckvn = (_rms(ckv_ref[0]) * gkv_ref[...]).astype(BF16)
    k = _dot(ckvn, wk_ref[...])
    for h in range(MLA_HEADS):
        base = h * MLA_QW
        q_out[0, :, base:base + MLA_NOPE] = (q[:, base:base + MLA_NOPE] * scale).astype(q_out.dtype)
        q_out[0, :, base + MLA_NOPE:base + MLA_QW] = (_rotate(q[:, base + MLA_NOPE:base + MLA_QW], cs) * scale).astype(q_out.dtype)
        k_out[0, :, base:base + MLA_NOPE] = k[:, h * MLA_NOPE:(h + 1) * MLA_NOPE].astype(k_out.dtype)
        k_out[0, :, base + MLA_NOPE:base + MLA_QW] = k_rope
    vt_out[0] = _dot_nt(wvt_ref[...], ckvn).astype(vt_out.dtype)


def _mla_proj(proj3, cs3, cq_g, ckv_g, w_uq, w_uk, w_uvt, *, tm=512):
    b, s, _ = proj3.shape
    tm = min(tm, s)
    nq = w_uq.shape[1]
    nk = MLA_HEADS * MLA_QW
    nv = w_uvt.shape[0]
    vmem = (2 * tm * (MLA_Q_RANK + MLA_KV_RANK + 2 * LANES) * 4 + 2 * MLA_Q_RANK * (nq + w_uk.shape[1] + nv) * 2
            + 2 * tm * (nq + nk + nv) * 2 + 3 * tm * (nq + nk + nv) * 4)
    const = lambda shape: pl.BlockSpec(shape, lambda bi, i: (0, 0))
    return pl.pallas_call(
        _mla_proj_body,
        grid=(b, s // tm),
        in_specs=[
            pl.BlockSpec((1, tm, MLA_Q_RANK), lambda bi, i: (bi, i, OFF_CQ // MLA_Q_RANK)),
            pl.BlockSpec((1, tm, MLA_KV_RANK), lambda bi, i: (bi, i, OFF_CKV // MLA_KV_RANK)),
            pl.BlockSpec((1, tm, LANES), lambda bi, i: (bi, i, OFF_KR // LANES)),
            pl.BlockSpec((1, tm, LANES), lambda bi, i: (bi, i, 0)),
            const((1, MLA_Q_RANK)), const((1, MLA_KV_RANK)),
            const(w_uq.shape), const(w_uk.shape), const(w_uvt.shape),
        ],
        out_specs=[
            pl.BlockSpec((1, tm, nq), lambda bi, i: (bi, i, 0)),
            pl.BlockSpec((1, tm, nk), lambda bi, i: (bi, i, 0)),
            pl.BlockSpec((1, nv, tm), lambda bi, i: (bi, 0, i)),
        ],
        out_shape=[jax.ShapeDtypeStruct((b, s, nq), BF16), jax.ShapeDtypeStruct((b, s, nk), BF16),
                   jax.ShapeDtypeStruct((b, nv, s), BF16)],
        compiler_params=_cparams(("parallel", "parallel"), vmem),
        name="mla_proj",
    )(proj3, proj3, proj3, cs3, cq_g.reshape(1, -1), ckv_g.reshape(1, -1), w_uq, w_uk, w_uvt)


ATTN_HEADS_PER_STEP = 8


def _attn_body(q_ref, k_ref, vt_ref, o_ref, *, tq, nh):
    i = pl.program_id(2)
    diag_mask = (lax.broadcasted_iota(jnp.int32, (tq, tq), 0) // CHUNK
                 <= lax.broadcasted_iota(jnp.int32, (tq, tq), 1) // CHUNK)

    def kv_step(j, carry, masked):
        k0 = pl.multiple_of(j * tq, tq)
        heads = range(nh)
        st = [_dot_nt(k_ref[0, pl.ds(k0, tq), h * MLA_QW:(h + 1) * MLA_QW], q_ref[0, :, h * MLA_QW:(h + 1) * MLA_QW])
              for h in heads]
        if masked:
            st = [jnp.where(diag_mask, x, -1e30) for x in st]
        m_new = [jnp.maximum(carry[h][0], jnp.max(st[h], axis=0, keepdims=True)) for h in heads]
        alpha = [jnp.exp(carry[h][0] - m_new[h]) for h in heads]
        pt = [jnp.exp(st[h] - m_new[h]) for h in heads]
        l = [alpha[h] * carry[h][1] + jnp.sum(pt[h], axis=0, keepdims=True) for h in heads]
        pv = [_dot(vt_ref[0, h * MLA_V:(h + 1) * MLA_V, pl.ds(k0, tq)], pt[h].astype(BF16)) for h in heads]
        return tuple((m_new[h], l[h], alpha[h] * carry[h][2] + pv[h]) for h in heads)

    init = tuple((jnp.full((1, tq), -jnp.inf, F32), jnp.zeros((1, tq), F32), jnp.zeros((MLA_V, tq), F32))
                 for _ in range(nh))
    carry = lax.fori_loop(0, i, functools.partial(kv_step, masked=False), init)
    carry = kv_step(i, carry, masked=True)
    for h in range(nh):
        _, l, acc = carry[h]
        o_ref[0, :, h * MLA_V:(h + 1) * MLA_V] = (acc / l).T.astype(o_ref.dtype)


def _attention(q3, k3, vt3, *, tq=256, nh=ATTN_HEADS_PER_STEP):
    b, s, _ = q3.shape
    tq = min(tq, s)
    vmem = (2 * tq * nh * MLA_QW * 2 + 2 * s * nh * MLA_QW * 2 + 2 * s * nh * MLA_V * 2 + 2 * tq * nh * MLA_V * 2
            + nh * 8 * tq * tq * 4)
    return pl.pallas_call(
        functools.partial(_attn_body, tq=tq, nh=nh),
        grid=(b, MLA_HEADS // nh, s // tq),
        in_specs=[
            pl.BlockSpec((1, tq, nh * MLA_QW), lambda bi, hg, i: (bi, i, hg)),
            pl.BlockSpec((1, s, nh * MLA_QW), lambda bi, hg, i: (bi, 0, hg)),
            pl.BlockSpec((1, nh * MLA_V, s), lambda bi, hg, i: (bi, hg, 0)),
        ],
        out_specs=pl.BlockSpec((1, tq, nh * MLA_V), lambda bi, hg, i: (bi, i, hg)),
        out_shape=jax.ShapeDtypeStruct((b, s, MLA_HEADS * MLA_V), BF16),
        compiler_params=_cparams(("parallel", "parallel", "arbitrary"), vmem),
        name="mla_attention",
    )(q3, k3, vt3)


def _merge_body(x_ref, oa_ref, ob_ref, oc_ref, ga_ref, gb_ref, gc_ref, wa_ref, wb_ref, wc_ref, wo_ref, g_ref,
                o_ref, *, nc):
    c = pl.program_id(1)

    @pl.when(c == 0)
    def _():
        o_ref[...] = jnp.zeros_like(o_ref)

    merged = jax.nn.sigmoid(ga_ref[...]) * _dot(oa_ref[...], wa_ref[...])
    merged = merged + jax.nn.sigmoid(gb_ref[...]) * _dot(ob_ref[...], wb_ref[...])
    merged = merged + jax.nn.sigmoid(gc_ref[...]) * _dot(oc_ref[...], wc_ref[...])
    o_ref[...] += _dot(merged.astype(BF16), wo_ref[...])

    @pl.when(c == nc - 1)
    def _():
        o_ref[...] = x_ref[...] + _rms(o_ref[...]) * g_ref[...]


def _merge(x, oa, ob, oc, proj, w_branch, w_out, l, g, *, tm=512, tc=512):
    t, d = x.shape
    tm = min(tm, t)
    nc = d // tc
    bw = oa.shape[1]
    gate_blk = OFF_GATE // tc
    vmem = (2 * tm * d * 4 + 2 * tm * d * 4 + 2 * 3 * tm * bw * 2 + 2 * 3 * tm * tc * 4 + 2 * 3 * bw * tc * 2
            + 2 * tc * d * 2 + 6 * tm * tc * 4 + tm * d * 4)
    o_spec = pl.BlockSpec((tm, bw), lambda i, c: (i, 0))
    gate_spec = lambda n: pl.BlockSpec((tm, tc), lambda i, c: (i, gate_blk + n * nc + c))
    wb_spec = lambda n: pl.BlockSpec((None, bw, tc), lambda i, c: (N_BRANCH * l + n, 0, c))
    return pl.pallas_call(
        functools.partial(_merge_body, nc=nc),
        grid=(t // tm, nc),
        in_specs=[pl.BlockSpec((tm, d), lambda i, c: (i, 0)), o_spec, o_spec, o_spec,
                  gate_spec(0), gate_spec(1), gate_spec(2), wb_spec(0), wb_spec(1), wb_spec(2),
                  pl.BlockSpec((None, tc, d), lambda i, c: (l, c, 0)),
                  pl.BlockSpec((1, d), lambda i, c: (0, 0))],
        out_specs=pl.BlockSpec((tm, d), lambda i, c: (i, 0)),
        out_shape=jax.ShapeDtypeStruct((t, d), F32),
        compiler_params=_cparams(("parallel", "arbitrary"), vmem),
        name="merge",
    )(x, oa, ob, oc, proj, proj, proj, w_branch, w_branch, w_branch, w_out, g.reshape(1, d))


def kernel(x, positions, norm_g, ffn_w_gate, ffn_w_up, ffn_w_down, w_in, dn_conv_w, dn_a_log, dn_dt_bias, dn_norm_g, sg_ln_g, sg_ln_b, sg_w, sg_b, mla_cq_norm_g, mla_ckv_norm_g, mla_w_uq, mla_w_ukv, w_branch, w_out):
    b, s, d = x.shape
    t = b * s
    xf = x.reshape(t, d)
    cs = _rope_table(positions)
    wg_all, wu_all, wd_all = _cast_bf16(ffn_w_gate), _cast_bf16(ffn_w_up), _cast_bf16(ffn_w_down)
    wb_all, wo_all = _cast_bf16(w_branch), _cast_bf16(w_out)
    for l in range(DEPTH):
        ng = norm_g[l]
        xf = _ffn(xf, ng[0], wg_all, wu_all, wd_all, 2 * l, ng[1])
        proj = _norm_proj(xf, ng[2], _relayout_w_in(w_in, l))
        proj3 = proj.reshape(b, s, PROJ_COLS)
        o_a = _deltanet(proj3, dn_conv_w[l], dn_a_log[l], dn_dt_bias[l], dn_norm_g[l])
        o_b = _gmlp(proj3, sg_ln_g[l], sg_ln_b[l], sg_w[l], sg_b[l])
        w_uk, w_uvt = _split_w_ukv(mla_w_ukv[l])
        q, k, vt = _mla_proj(proj3, cs.reshape(b, s, LANES), mla_cq_norm_g[l], mla_ckv_norm_g[l],
                             _relayout_w_uq(mla_w_uq[l]), w_uk, w_uvt)
        o_c = _attention(q, k, vt)
        xf = _merge(xf, o_a.reshape(t, -1), o_b.reshape(t, -1), o_c.reshape(t, -1), proj, wb_all, wo_all, l, ng[3])
        xf = _ffn(xf, ng[4], wg_all, wu_all, wd_all, 2 * l + 1, ng[5])
    return xf.reshape(b, s, d)
```

```python
import functools

import jax
import jax.numpy as jnp
from jax import lax
from jax.experimental import pallas as pl
from jax.experimental.pallas import tpu as pltpu

F32 = jnp.float32
BF16 = jnp.bfloat16

D_MODEL = 2048
DEPTH = 2
CHUNK = 64
NORM_EPS = 1e-6
DN_HEADS = 8
DN_HEAD_DIM = 128
DN_WIDTH = DN_HEADS * DN_HEAD_DIM
DN_CONV = 4
SG_GROUPS = 8
SG_GROUP_DIM = 128
SG_WIDTH = SG_GROUPS * SG_GROUP_DIM
SG_BLOCK = 128
MLA_HEADS = 8
MLA_Q_RANK = 512
MLA_KV_RANK = 512
MLA_NOPE = 128
MLA_ROPE = 64
MLA_V = 128
ROPE_THETA = 10000.0
N_BRANCH = 3
BRANCH_WIDTH = 1024
D_FF = 5632

LANES = 128
SUBLANES = 8
V7X_VMEM_BYTES = 64 * 1024 * 1024
VMEM_CAP = V7X_VMEM_BYTES - 6 * 1024 * 1024

OFF_QKV = 0
OFF_Z = OFF_QKV + 3 * DN_WIDTH
OFF_SG = OFF_Z + DN_WIDTH
OFF_GATE = OFF_SG + 2 * SG_WIDTH
OFF_CQ = OFF_GATE + N_BRANCH * D_MODEL
OFF_CKV = OFF_CQ + MLA_Q_RANK
OFF_AB = OFF_CKV + MLA_KV_RANK
OFF_KR = OFF_AB + LANES
PROJ_TILE = 512
PROJ_COLS = -(-(OFF_KR + LANES) // PROJ_TILE) * PROJ_TILE
PROJ_TAIL = PROJ_COLS - OFF_AB
GATE_COLS = N_BRANCH * D_MODEL
ACT_COLS = PROJ_COLS - GATE_COLS
ACT_CQ = OFF_CQ - GATE_COLS
ACT_CKV = OFF_CKV - GATE_COLS
ACT_AB = OFF_AB - GATE_COLS
ACT_KR = OFF_KR - GATE_COLS
MLA_QW = 2 * LANES

SRC_AB = 4 * DN_WIDTH
SRC_SG = SRC_AB + 2 * DN_HEADS
SRC_CQ = SRC_SG + 2 * SG_WIDTH
SRC_CKV = SRC_CQ + MLA_Q_RANK
SRC_KR = SRC_CKV + MLA_KV_RANK
SRC_GATE = SRC_KR + MLA_ROPE
IN_COLS = SRC_GATE + N_BRANCH * D_MODEL


def _cparams(semantics, vmem_bytes):
    return pltpu.CompilerParams(dimension_semantics=semantics,
                                vmem_limit_bytes=int(min(max(vmem_bytes, 16 * 1024 * 1024), VMEM_CAP)))


def _dot(a, b):
    return jnp.dot(a, b, preferred_element_type=F32)


def _dot_nt(a, b):
    return lax.dot_general(a, b, (((1,), (1,)), ((), ())), preferred_element_type=F32)


def _dot_tn(a, b):
    return lax.dot_general(a, b, (((0,), (0,)), ((), ())), preferred_element_type=F32)


def _rms(x, eps=NORM_EPS):
    return x * lax.rsqrt(jnp.mean(x * x, axis=-1, keepdims=True) + eps)


def _silu(x):
    return x * jax.nn.sigmoid(x)


def _cast_body(w_ref, o_ref):
    o_ref[...] = w_ref[...].astype(o_ref.dtype)


def _cast_bf16(w, only=None, *, block_bytes=4 * 1024 * 1024):
    r, c = w.shape[-2:]
    w = w.reshape(-1, r, c)
    n, first = (w.shape[0], 0) if only is None else (1, only)
    rb = r
    while rb * c * 4 > block_bytes and rb % (4 * SUBLANES) == 0:
        rb //= 2
    return pl.pallas_call(
        _cast_body,
        grid=(n, r // rb),
        in_specs=[pl.BlockSpec((1, rb, c), lambda m, i: (first + m, i, 0))],
        out_specs=pl.BlockSpec((1, rb, c), lambda m, i: (m, i, 0)),
        out_shape=jax.ShapeDtypeStruct((n, r, c), BF16),
        compiler_params=_cparams(("parallel", "parallel"), 3 * 2 * rb * c * 4),
        name="cast_bf16",
    )(w)


_WIDE_SEGMENTS = ((OFF_QKV, 0, SRC_AB), (OFF_SG, SRC_SG, 2 * SG_WIDTH), (OFF_GATE, SRC_GATE, N_BRANCH * D_MODEL),
                  (OFF_CQ, SRC_CQ, MLA_Q_RANK), (OFF_CKV, SRC_CKV, MLA_KV_RANK))


def _relayout_body(wt_ref, tail_ref, o_ref, *, n_wide):
    i = pl.program_id(0)

    @pl.when(i < n_wide)
    def _():
        o_ref[...] = wt_ref[0].T.astype(BF16)

    @pl.when(i >= n_wide)
    def _():
        o_ref[...] = tail_ref[...].astype(BF16)


def _swap_halves(w):
    half = w.shape[-1] // 2
    return jnp.concatenate([-w[..., half:], w[..., :half]], axis=-1)


def _relayout_w_in(w_in, l):
    d = w_in.shape[1]
    tb = PROJ_TAIL
    kr = w_in[l, :, SRC_KR:SRC_GATE]
    tail = jnp.concatenate([w_in[l, :, SRC_AB:SRC_SG], jnp.zeros((d, LANES - 2 * DN_HEADS), F32), kr, _swap_halves(kr),
                            jnp.zeros((d, PROJ_TAIL - 2 * LANES), F32)], axis=1)
    n_wide = OFF_AB // tb

    def src_start(i):
        dst = i * tb
        start = 0
        for d0, s0, width in _WIDE_SEGMENTS:
            start = jnp.where((dst >= d0) & (dst < d0 + width), s0 + dst - d0, start)
        return pl.multiple_of(start, 2 * SUBLANES)

    return pl.pallas_call(
        functools.partial(_relayout_body, n_wide=n_wide),
        grid=(PROJ_COLS // tb,),
        in_specs=[pl.BlockSpec((pl.Element(1), pl.Element(tb), pl.Element(d)), lambda i: (l, src_start(i), 0)),
                  pl.BlockSpec((d, tb), lambda i: (0, 0))],
        out_specs=pl.BlockSpec((d, tb), lambda i: (0, i)),
        out_shape=jax.ShapeDtypeStruct((d, PROJ_COLS), BF16),
        compiler_params=_cparams(("parallel",), 2 * tb * d * 4 + 2 * tb * d * 4 + 2 * tb * d * 2 + 2 * tb * d * 4),
        name="relayout_w_in",
    )(jnp.swapaxes(w_in, 1, 2), tail)


def _relayout_w_uq(w):
    r = w.shape[0]
    w = w.reshape(r, MLA_HEADS, MLA_NOPE + MLA_ROPE)
    rope = w[..., MLA_NOPE:]
    return jnp.concatenate([w[..., :MLA_NOPE], rope, _swap_halves(rope)], axis=-1).reshape(r, MLA_HEADS * MLA_QW).astype(BF16)


def _split_w_ukv(w):
    r = w.shape[0]
    w = w.reshape(r, MLA_HEADS, MLA_NOPE + MLA_V)
    w_uk = w[..., :MLA_NOPE].reshape(r, MLA_HEADS * MLA_NOPE).astype(BF16)
    w_uvt = w[..., MLA_NOPE:].reshape(r, MLA_HEADS * MLA_V).T.astype(BF16)
    return w_uk, w_uvt


def _ffn_body(x_ref, gpre_ref, wg_ref, wu_ref, wd_ref, gpost_ref, *rest, nf, n_cast):
    cast_src, o_ref, cast_dst, xn_ref = rest[:n_cast], rest[n_cast], rest[n_cast + 1:2 * n_cast + 1], rest[-1]
    f = pl.program_id(1)

    @pl.when(f == 0)
    def _():
        xn_ref[...] = (_rms(x_ref[...]) * gpre_ref[...]).astype(BF16)
        o_ref[...] = jnp.zeros_like(o_ref)

    xn = xn_ref[...]
    h = _dot(xn, wg_ref[...])
    u = _dot(xn, wu_ref[...])
    a = (_silu(h) * u).astype(BF16)
    o_ref[...] += _dot(a, wd_ref[...])
    for src, dst in zip(cast_src, cast_dst):
        dst[...] = src[...].astype(dst.dtype)

    @pl.when(f == nf - 1)
    def _():
        o_ref[...] = x_ref[...] + 0.5 * (_rms(o_ref[...]) * gpost_ref[...])


def _ffn(x, g_pre, w_gate, w_up, w_down, wi, g_post, cast_next=None, *, tm=512, tf=512):
    t, d = x.shape
    tm = min(tm, t)
    dff = w_gate.shape[-1]
    nf = dff // tf
    nt = t // tm
    vmem = (2 * tm * d * 4 + 2 * tm * d * 4 + tm * d * 2 + 2 * 3 * d * tf * 2 + 5 * tm * tf * 4 + tm * d * 4)
    in_specs = [
        pl.BlockSpec((tm, d), lambda i, f: (i, 0)),
        pl.BlockSpec((1, d), lambda i, f: (0, 0)),
        pl.BlockSpec((None, d, tf), lambda i, f: (wi, 0, f)),
        pl.BlockSpec((None, d, tf), lambda i, f: (wi, 0, f)),
        pl.BlockSpec((None, tf, d), lambda i, f: (wi, f, 0)),
        pl.BlockSpec((1, d), lambda i, f: (0, 0)),
    ]
    out_specs = [pl.BlockSpec((tm, d), lambda i, f: (i, 0))]
    out_shape = [jax.ShapeDtypeStruct((t, d), F32)]
    operands = [x, g_pre.reshape(1, d), w_gate, w_up, w_down, g_post.reshape(1, d)]
    n_cast = 0
    if cast_next is not None:
        *srcs, ci = cast_next
        n_cast = len(srcs)
        dr = d // nt
        in_specs += [pl.BlockSpec((None, dr, tf), lambda i, f: (ci, i, f)),
                     pl.BlockSpec((None, dr, tf), lambda i, f: (ci, i, f)),
                     pl.BlockSpec((None, tf, dr), lambda i, f: (ci, f, i))]
        out_specs += [pl.BlockSpec((dr, tf), lambda i, f: (i, f)), pl.BlockSpec((dr, tf), lambda i, f: (i, f)),
                      pl.BlockSpec((tf, dr), lambda i, f: (f, i))]
        out_shape += [jax.ShapeDtypeStruct(s.shape[-2:], BF16) for s in srcs]
        operands += srcs
        vmem += 2 * 3 * dr * tf * (4 + 2)
    return pl.pallas_call(
        functools.partial(_ffn_body, nf=nf, n_cast=n_cast),
        grid=(nt, nf),
        in_specs=in_specs,
        out_specs=out_specs,
        out_shape=out_shape,
        scratch_shapes=[pltpu.VMEM((tm, d), BF16)],
        compiler_params=_cparams(("parallel", "arbitrary"), vmem),
        name="ffn",
    )(*operands)


def _proj_body(x_ref, g_ref, w_ref, o_ref, gate_ref, xn_ref, *, gate_lo, gate_hi):
    j = pl.program_id(1)

    @pl.when(j == 0)
    def _():
        xn_ref[...] = (_rms(x_ref[...]) * g_ref[...]).astype(BF16)

    res = _dot(xn_ref[...], w_ref[...])
    is_gate = (j >= gate_lo) & (j < gate_hi)

    @pl.when(is_gate)
    def _():
        gate_ref[...] = res.astype(gate_ref.dtype)

    @pl.when(jnp.logical_not(is_gate))
    def _():
        o_ref[...] = res


def _norm_proj(x, g, w, *, tm=1024, tn=3 * PROJ_TILE):
    t, d = x.shape
    n = w.shape[1]
    tm = min(tm, t)
    gate_lo, gate_hi = OFF_GATE // tn, (OFF_GATE + GATE_COLS) // tn
    n_gate = gate_hi - gate_lo
    act_tile = lambda j: jnp.where(j < gate_hi, jnp.minimum(j, gate_lo - 1), j - n_gate)
    gate_tile = lambda j: jnp.clip(j - gate_lo, 0, n_gate - 1)
    vmem = tm * d * 4 + tm * d * 2 + 2 * d * tn * 2 + 3 * tm * tn * 4 + 2 * tm * tn * 2 + tm * d * 4
    return pl.pallas_call(
        functools.partial(_proj_body, gate_lo=gate_lo, gate_hi=gate_hi),
        grid=(t // tm, n // tn),
        in_specs=[
            pl.BlockSpec((tm, d), lambda i, j: (i, 0), pipeline_mode=pl.Buffered(1)),
            pl.BlockSpec((1, d), lambda i, j: (0, 0)),
            pl.BlockSpec((d, tn), lambda i, j: (0, j)),
        ],
        out_specs=[pl.BlockSpec((tm, tn), lambda i, j: (i, act_tile(j))),
                   pl.BlockSpec((tm, tn), lambda i, j: (i, gate_tile(j)))],
        out_shape=[jax.ShapeDtypeStruct((t, n - GATE_COLS), F32), jax.ShapeDtypeStruct((t, GATE_COLS), BF16)],
        scratch_shapes=[pltpu.VMEM((tm, d), BF16)],
        compiler_params=_cparams(("parallel", "arbitrary"), vmem),
        name="norm_proj",
    )(x, g.reshape(1, d), w)


def _rope_body(pos_ref, inv_ref, o_ref):
    ang = pos_ref[...] * inv_ref[...]
    lane = lax.broadcasted_iota(jnp.int32, ang.shape, 1)
    o_ref[...] = jnp.where(lane < 2 * (MLA_ROPE // 2), jnp.cos(ang), jnp.sin(ang))


def _rope_table(positions):
    b, s = positions.shape
    t = b * s
    inv_freq = jnp.power(ROPE_THETA, -jnp.arange(0, MLA_ROPE, 2, dtype=F32) / MLA_ROPE)
    inv = jnp.tile(inv_freq, LANES // (MLA_ROPE // 2)).reshape(1, LANES)
    pos = jnp.broadcast_to(positions.astype(F32).reshape(t, 1), (t, LANES))
    tm = min(1024, t)
    return pl.pallas_call(
        _rope_body,
        grid=(t // tm,),
        in_specs=[pl.BlockSpec((tm, LANES), lambda i: (i, 0)), pl.BlockSpec((1, LANES), lambda i: (0, 0))],
        out_specs=pl.BlockSpec((tm, LANES), lambda i: (i, 0)),
        out_shape=jax.ShapeDtypeStruct((t, LANES), F32),
        compiler_params=_cparams(("parallel",), 0),
        name="rope_table",
    )(pos, inv)


DN_HALO = SUBLANES


def _dn_body(alog_ref, dtb_ref, q_ref, k_ref, v_ref, z_ref, ab_ref, wq_ref, wk_ref, wv_ref, ng_ref, o_ref,
             state_ref, halo_ref):
    sblk = q_ref.shape[1]
    hd = DN_HEAD_DIM

    @pl.when(pl.program_id(1) == 0)
    def _():
        state_ref[...] = jnp.zeros_like(state_ref)
        halo_ref[...] = jnp.zeros_like(halo_ref)

    lane = lax.broadcasted_iota(jnp.int32, (1, LANES), 1)
    row = lax.broadcasted_iota(jnp.int32, (CHUNK, CHUNK), 0)
    col = lax.broadcasted_iota(jnp.int32, (CHUNK, CHUNK), 1)
    tril = (row >= col).astype(F32)
    row2 = lax.broadcasted_iota(jnp.int32, (2 * CHUNK, 2 * CHUNK), 0)
    col2i = lax.broadcasted_iota(jnp.int32, (2 * CHUNK, 2 * CHUNK), 1)
    same_head = (row2 // CHUNK) == (col2i // CHUNK)
    incl2 = same_head & (row2 >= col2i)
    strict2 = same_head & (row2 > col2i)
    a_scale = jnp.where(lane < DN_HEADS, -jnp.exp(alog_ref[...]), 0.0)
    dt_bias = dtb_ref[...]
    ng = ng_ref[...]
    seq_refs = (q_ref, k_ref, v_ref)
    w_refs = (wq_ref, wk_ref, wv_ref)

    def conv_silu(which, sl, c, c0):
        ref = seq_refs[which]
        cur = ref[0, pl.ds(c0, CHUNK), sl]
        p0 = pl.multiple_of(jnp.maximum(c0 - DN_HALO, 0), DN_HALO)
        prev = jnp.where(c > 0, ref[0, pl.ds(p0, DN_HALO), sl], halo_ref[which, :, sl])
        xx = jnp.concatenate([prev, cur], axis=0)
        w = w_refs[which][:, sl]
        first = DN_HALO - (DN_CONV - 1)
        y = w[0:1] * xx[first:first + CHUNK]
        for j in range(1, DN_CONV):
            y = y + w[j:j + 1] * xx[first + j:first + j + CHUNK]
        return _silu(y)

    heads = range(DN_HEADS)
    pairs = range(DN_HEADS // 2)
    unroll = 4
    items = [(ci, pr) for ci in range(unroll) for pr in pairs]
    sls = [slice(h * hd, (h + 1) * hd) for h in heads]
    top, bot = slice(0, CHUNK), slice(CHUNK, 2 * CHUNK)
    half = lambda h: top if h % 2 == 0 else bot

    def col2(a, off, pr):
        return jnp.concatenate([a[:, off + 2 * pr:off + 2 * pr + 1], a[:, off + 2 * pr + 1:off + 2 * pr + 2]], axis=0)

    def gate_terms(c0):
        ab = ab_ref[0, pl.ds(c0, CHUNK), :]
        xa = ab + dt_bias
        g_all = a_scale * (jnp.maximum(xa, 0.0) + jnp.log1p(jnp.exp(-jnp.abs(xa))))
        beta_all = jax.nn.sigmoid(ab)
        gc_all = jnp.dot(tril, g_all, precision=lax.Precision.HIGHEST, preferred_element_type=F32)
        g_last = gc_all[CHUNK - 1:CHUNK, :]
        gc_t = jnp.concatenate([gc_all, gc_all], axis=0).T
        return dict(beta=beta_all, gc=gc_all, gc_t=gc_t, eg=jnp.exp(gc_all), ed=jnp.exp(g_last - gc_all),
                    egl=jnp.exp(g_last))

    def chunk_group_step(it, carry):
        cidx = [it * unroll + ci for ci in range(unroll)]
        c0s = [pl.multiple_of(c * CHUNK, CHUNK) for c in cidx]
        gt = [gate_terms(c0) for c0 in c0s]

        def stacked(fn, ci, pr):
            return jnp.concatenate([fn(ci, 2 * pr), fn(ci, 2 * pr + 1)], axis=0)

        z2 = {(ci, pr): stacked(lambda ci, h: z_ref[0, pl.ds(c0s[ci], CHUNK), sls[h]], ci, pr) for ci, pr in items}
        q2 = {(ci, pr): stacked(lambda ci, h: conv_silu(0, sls[h], cidx[ci], c0s[ci]), ci, pr) for ci, pr in items}
        k2 = {(ci, pr): stacked(lambda ci, h: conv_silu(1, sls[h], cidx[ci], c0s[ci]), ci, pr) for ci, pr in items}
        v2 = {(ci, pr): stacked(lambda ci, h: conv_silu(2, sls[h], cidx[ci], c0s[ci]), ci, pr) for ci, pr in items}
        q2 = {i: x * lax.rsqrt(jnp.sum(x * x, axis=-1, keepdims=True) + 1e-6) * (hd ** -0.5) for i, x in q2.items()}
        k2 = {i: x * lax.rsqrt(jnp.sum(x * x, axis=-1, keepdims=True) + 1e-6) for i, x in k2.items()}
        beta2 = {(ci, pr): col2(gt[ci]["beta"], DN_HEADS, pr) for ci, pr in items}
        eg2 = {(ci, pr): col2(gt[ci]["eg"], 0, pr) for ci, pr in items}
        g_row2 = {(ci, pr): jnp.where(lane < CHUNK, gt[ci]["gc_t"][2 * pr:2 * pr + 1], gt[ci]["gc_t"][2 * pr + 1:2 * pr + 2])
                  for ci, pr in items}
        decay2 = {(ci, pr): jnp.exp(jnp.where(incl2, col2(gt[ci]["gc"], 0, pr) - g_row2[ci, pr], -jnp.inf))
                  for ci, pr in items}
        kb2 = {i: k2[i] * beta2[i] for i in items}
        km2 = {i: k2[i].astype(BF16) for i in items}
        kk2 = {i: _dot_nt(jnp.concatenate([kb2[i], q2[i]], axis=0).astype(BF16), km2[i]) for i in items}
        qk2 = {i: (kk2[i][2 * CHUNK:] * decay2[i]).astype(BF16) for i in items}
        p2 = {i: jnp.where(strict2, -(kk2[i][:2 * CHUNK] * decay2[i]), 0.0) for i in items}
        sol2 = {i: jnp.concatenate([kb2[i] * eg2[i], v2[i] * beta2[i]], axis=1) for i in items}
        sol2 = {i: sol2[i] + _dot(p2[i].astype(BF16), sol2[i].astype(BF16)) for i in items}
        for _ in range(5):
            pb = {i: p2[i].astype(BF16) for i in items}
            p2 = {i: _dot(pb[i], pb[i]) for i in items}
            sol2 = {i: sol2[i] + _dot(p2[i].astype(BF16), sol2[i].astype(BF16)) for i in items}
        wq2 = {(ci, h): jnp.concatenate([sol2[ci, h // 2][half(h), :hd], (q2[ci, h // 2] * eg2[ci, h // 2])[half(h)]],
                                        axis=0).astype(BF16) for ci in range(unroll) for h in heads}
        kd2 = {(ci, pr): (k2[ci, pr] * col2(gt[ci]["ed"], 0, pr)).astype(BF16) for ci, pr in items}
        states = [state_ref[h] for h in heads]
        for ci in range(unroll):
            ws = [_dot(wq2[ci, h], states[h].astype(BF16)) for h in heads]
            vn2 = [(sol2[ci, pr][:, hd:] - jnp.concatenate([ws[2 * pr][:CHUNK], ws[2 * pr + 1][:CHUNK]], axis=0)).astype(BF16)
                   for pr in pairs]
            states = [states[h] * gt[ci]["egl"][:, h:h + 1] + _dot_tn(kd2[ci, h // 2][half(h)], vn2[h // 2][half(h)])
                      for h in heads]
            o2 = [jnp.concatenate([ws[2 * pr][CHUNK:], ws[2 * pr + 1][CHUNK:]], axis=0) + _dot(qk2[ci, pr], vn2[pr])
                  for pr in pairs]
            out2 = [(_rms(o2[pr]) * ng * _silu(z2[ci, pr])).astype(o_ref.dtype) for pr in pairs]
            for h in heads:
                o_ref[0, pl.ds(c0s[ci], CHUNK), sls[h]] = out2[h // 2][half(h)]
        for h in heads:
            state_ref[h] = states[h]
        return carry

    lax.fori_loop(0, sblk // (CHUNK * unroll), chunk_group_step, 0)
    for which in range(3):
        halo_ref[which] = seq_refs[which][0, sblk - DN_HALO:sblk, :]


def _deltanet(proj3, conv_w, a_log, dt_bias, norm_g, *, sblk=512):
    b, s, _ = proj3.shape
    sblk = min(sblk, s)
    wd = DN_WIDTH
    pad = lambda vec: jnp.concatenate([vec, jnp.zeros((LANES - DN_HEADS,), F32)]).reshape(1, LANES)
    seq_spec = lambda blk: pl.BlockSpec((1, sblk, wd), lambda i, j: (i, j, blk))
    w_spec = lambda blk: pl.BlockSpec((DN_CONV, wd), lambda i, j: (0, blk))
    vec_spec = pl.BlockSpec((1, LANES), lambda i, j: (0, 0))
    live_chunk_values = 16 * 1024 * 1024
    vmem = 2 * 4 * sblk * wd * 4 + 2 * sblk * wd * 2 + 2 * sblk * LANES * 4 + live_chunk_values
    return pl.pallas_call(
        _dn_body,
        grid=(b, s // sblk),
        in_specs=[vec_spec, vec_spec,
                  seq_spec(OFF_QKV // wd), seq_spec(OFF_QKV // wd + 1), seq_spec(OFF_QKV // wd + 2), seq_spec(OFF_Z // wd),
                  pl.BlockSpec((1, sblk, LANES), lambda i, j: (i, j, ACT_AB // LANES)),
                  w_spec(0), w_spec(1), w_spec(2), vec_spec],
        out_specs=pl.BlockSpec((1, sblk, wd), lambda i, j: (i, j, 0)),
        out_shape=jax.ShapeDtypeStruct((b, s, wd), BF16),
        scratch_shapes=[pltpu.VMEM((DN_HEADS, DN_HEAD_DIM, DN_HEAD_DIM), F32), pltpu.VMEM((3, DN_HALO, wd), F32)],
        compiler_params=_cparams(("parallel", "arbitrary"), vmem),
        name="deltanet",
    )(pad(a_log), pad(dt_bias), proj3, proj3, proj3, proj3, proj3, conv_w, conv_w, conv_w, norm_g.reshape(1, -1))


def _gelu(x):
    return 0.5 * x * (1.0 + lax.erf(x * (0.5 ** 0.5)))


def _sg_body(u_ref, v_ref, lng_ref, lnb_ref, w_ref, bt_ref, o_ref):
    u = _gelu(u_ref[0])
    v = _gelu(v_ref[0])
    mu = jnp.mean(v, axis=-1, keepdims=True)
    vc = v - mu
    var = jnp.mean(vc * vc, axis=-1, keepdims=True)
    vn = (vc * lax.rsqrt(var + 1e-5) * lng_ref[...] + lnb_ref[...]).astype(BF16)
    row = lax.broadcasted_iota(jnp.int32, (SG_BLOCK, SG_BLOCK), 0)
    col = lax.broadcasted_iota(jnp.int32, (SG_BLOCK, SG_BLOCK), 1)
    mask = (col // CHUNK) <= (row // CHUNK)
    for g in range(SG_GROUPS):
        sl = slice(g * SG_GROUP_DIM, (g + 1) * SG_GROUP_DIM)
        wg = jnp.where(mask, w_ref[g], 0.0).astype(BF16)
        mixed = _dot(wg, vn[:, sl]) + bt_ref[:, g:g + 1]
        o_ref[0, :, sl] = (u[:, sl] * mixed).astype(o_ref.dtype)


def _gmlp(proj3, ln_g, ln_b, sg_w, sg_b):
    b, s, _ = proj3.shape
    ub = OFF_SG // SG_WIDTH
    vmem = 2 * 2 * SG_BLOCK * SG_WIDTH * 4 + 2 * SG_GROUPS * SG_BLOCK * SG_BLOCK * 4 + 8 * SG_BLOCK * SG_WIDTH * 4
    return pl.pallas_call(
        _sg_body,
        grid=(b, s // SG_BLOCK),
        in_specs=[
            pl.BlockSpec((1, SG_BLOCK, SG_WIDTH), lambda i, n: (i, n, ub)),
            pl.BlockSpec((1, SG_BLOCK, SG_WIDTH), lambda i, n: (i, n, ub + 1)),
            pl.BlockSpec((1, SG_WIDTH), lambda i, n: (0, 0)),
            pl.BlockSpec((1, SG_WIDTH), lambda i, n: (0, 0)),
            pl.BlockSpec((SG_GROUPS, SG_BLOCK, SG_BLOCK), lambda i, n: (0, 0, 0)),
            pl.BlockSpec((SG_BLOCK, SG_GROUPS), lambda i, n: (0, 0)),
        ],
        out_specs=pl.BlockSpec((1, SG_BLOCK, SG_WIDTH), lambda i, n: (i, n, 0)),
        out_shape=jax.ShapeDtypeStruct((b, s, SG_WIDTH), BF16),
        compiler_params=_cparams(("parallel", "parallel"), vmem),
        name="gmlp",
    )(proj3, proj3, ln_g.reshape(1, SG_WIDTH), ln_b.reshape(1, SG_WIDTH), sg_w, sg_b.T)


def _rotate(block, cs):
    t = block * cs
    return t + pltpu.roll(t, MLA_ROPE, 1)


def _mla_proj_body(cq_ref, ckv_ref, kr_ref, cs_ref, gq_ref, gkv_ref, wq_ref, wk_ref, wvt_ref, q_out, k_out, vt_out):
    cs = cs_ref[0]
    scale = (MLA_NOPE + MLA_ROPE) ** -0.5
    cqn = (_rms(cq_ref[0]) * gq_ref[...]).astype(BF16)
    q = _dot(cqn, wq_ref[...])
    lane = lax.broadcasted_iota(jnp.int32, cs.shape, 1)
    k_rope = jnp.where(lane < MLA_ROPE, _rotate(kr_ref[0], cs), 0.0).astype(k_out.dtype)
    ckvn = (_rms(ckv_ref[0]) * gkv_ref[...]).astype(BF16)
    k = _dot(ckvn, wk_ref[...])
    for h in range(MLA_HEADS):
        base = h * MLA_QW
        q_out[0, :, base:base + MLA_NOPE] = (q[:, base:base + MLA_NOPE] * scale).astype(q_out.dtype)
        q_out[0, :, base + MLA_NOPE:base + MLA_QW] = (_rotate(q[:, base + MLA_NOPE:base + MLA_QW], cs) * scale).astype(q_out.dtype)
        k_out[0, :, base:base + MLA_NOPE] = k[:, h * MLA_NOPE:(h + 1) * MLA_NOPE].astype(k_out.dtype)
        k_out[0, :, base + MLA_NOPE:base + MLA_QW] = k_rope
    vt_out[0] = _dot_nt(wvt_ref[...], ckvn).astype(vt_out.dtype)


def _mla_proj(proj3, cs3, cq_g, ckv_g, w_uq, w_uk, w_uvt, *, tm=512):
    b, s, _ = proj3.shape
    tm = min(tm, s)
    nq = w_uq.shape[1]
    nk = MLA_HEADS * MLA_QW
    nv = w_uvt.shape[0]
    vmem = (2 * tm * (MLA_Q_RANK + MLA_KV_RANK + 2 * LANES) * 4 + 2 * MLA_Q_RANK * (nq + w_uk.shape[1] + nv) * 2
            + 2 * tm * (nq + nk + nv) * 2 + 3 * tm * (nq + nk + nv) * 4)
    const = lambda shape: pl.BlockSpec(shape, lambda bi, i: (0, 0))
    return pl.pallas_call(
        _mla_proj_body,
        grid=(b, s // tm),
        in_specs=[
            pl.BlockSpec((1, tm, MLA_Q_RANK), lambda bi, i: (bi, i, ACT_CQ // MLA_Q_RANK)),
            pl.BlockSpec((1, tm, MLA_KV_RANK), lambda bi, i: (bi, i, ACT_CKV // MLA_KV_RANK)),
            pl.BlockSpec((1, tm, LANES), lambda bi, i: (bi, i, ACT_KR // LANES)),
            pl.BlockSpec((1, tm, LANES), lambda bi, i: (bi, i, 0)),
            const((1, MLA_Q_RANK)), const((1, MLA_KV_RANK)),
            const(w_uq.shape), const(w_uk.shape), const(w_uvt.shape),
        ],
        out_specs=[
            pl.BlockSpec((1, tm, nq), lambda bi, i: (bi, i, 0)),
            pl.BlockSpec((1, tm, nk), lambda bi, i: (bi, i, 0)),
            pl.BlockSpec((1, nv, tm), lambda bi, i: (bi, 0, i)),
        ],
        out_shape=[jax.ShapeDtypeStruct((b, s, nq), BF16), jax.ShapeDtypeStruct((b, s, nk), BF16),
                   jax.ShapeDtypeStruct((b, nv, s), BF16)],
        compiler_params=_cparams(("parallel", "parallel"), vmem),
        name="mla_proj",
    )(proj3, proj3, proj3, cs3, cq_g.reshape(1, -1), ckv_g.reshape(1, -1), w_uq, w_uk, w_uvt)


ATTN_HEADS_PER_STEP = 8


def _attn_body(q_ref, k_ref, vt_ref, o_ref, *, tq, nh):
    i = pl.program_id(2)
    diag_mask = (lax.broadcasted_iota(jnp.int32, (tq, tq), 0) // CHUNK
                 <= lax.broadcasted_iota(jnp.int32, (tq, tq), 1) // CHUNK)

    def kv_step(j, carry, masked):
        k0 = pl.multiple_of(j * tq, tq)
        heads = range(nh)
        st = [_dot_nt(k_ref[0, pl.ds(k0, tq), h * MLA_QW:(h + 1) * MLA_QW], q_ref[0, :, h * MLA_QW:(h + 1) * MLA_QW])
              for h in heads]
        if masked:
            st = [jnp.where(diag_mask, x, -1e30) for x in st]
        m_new = [jnp.maximum(carry[h][0], jnp.max(st[h], axis=0, keepdims=True)) for h in heads]
        alpha = [jnp.exp(carry[h][0] - m_new[h]) for h in heads]
        pt = [jnp.exp(st[h] - m_new[h]) for h in heads]
        l = [alpha[h] * carry[h][1] + jnp.sum(pt[h], axis=0, keepdims=True) for h in heads]
        pv = [_dot(vt_ref[0, h * MLA_V:(h + 1) * MLA_V, pl.ds(k0, tq)], pt[h].astype(BF16)) for h in heads]
        return tuple((m_new[h], l[h], alpha[h] * carry[h][2] + pv[h]) for h in heads)

    init = tuple((jnp.full((1, tq), -jnp.inf, F32), jnp.zeros((1, tq), F32), jnp.zeros((MLA_V, tq), F32))
                 for _ in range(nh))
    carry = lax.fori_loop(0, i, functools.partial(kv_step, masked=False), init)
    carry = kv_step(i, carry, masked=True)
    for h in range(nh):
        _, l, acc = carry[h]
        o_ref[0, :, h * MLA_V:(h + 1) * MLA_V] = (acc / l).T.astype(o_ref.dtype)


def _attention(q3, k3, vt3, *, tq=256, nh=ATTN_HEADS_PER_STEP):
    b, s, _ = q3.shape
    tq = min(tq, s)
    vmem = (2 * tq * nh * MLA_QW * 2 + 2 * s * nh * MLA_QW * 2 + 2 * s * nh * MLA_V * 2 + 2 * tq * nh * MLA_V * 2
            + nh * 8 * tq * tq * 4)
    return pl.pallas_call(
        functools.partial(_attn_body, tq=tq, nh=nh),
        grid=(b, MLA_HEADS // nh, s // tq),
        in_specs=[
            pl.BlockSpec((1, tq, nh * MLA_QW), lambda bi, hg, i: (bi, i, hg)),
            pl.BlockSpec((1, s, nh * MLA_QW), lambda bi, hg, i: (bi, 0, hg)),
            pl.BlockSpec((1, nh * MLA_V, s), lambda bi, hg, i: (bi, hg, 0)),
        ],
        out_specs=pl.BlockSpec((1, tq, nh * MLA_V), lambda bi, hg, i: (bi, i, hg)),
        out_shape=jax.ShapeDtypeStruct((b, s, MLA_HEADS * MLA_V), BF16),
        compiler_params=_cparams(("parallel", "parallel", "arbitrary"), vmem),
        name="mla_attention",
    )(q3, k3, vt3)


def _merge_body(x_ref, oa_ref, ob_ref, oc_ref, ga_ref, gb_ref, gc_ref, wa_ref, wb_ref, wc_ref, wo_ref, g_ref,
                o_ref, *, nc):
    c = pl.program_id(1)

    @pl.when(c == 0)
    def _():
        o_ref[...] = jnp.zeros_like(o_ref)

    merged = jax.nn.sigmoid(ga_ref[...]) * _dot(oa_ref[...], wa_ref[...])
    merged = merged + jax.nn.sigmoid(gb_ref[...]) * _dot(ob_ref[...], wb_ref[...])
    merged = merged + jax.nn.sigmoid(gc_ref[...]) * _dot(oc_ref[...], wc_ref[...])
    o_ref[...] += _dot(merged.astype(BF16), wo_ref[...])

    @pl.when(c == nc - 1)
    def _():
        o_ref[...] = x_ref[...] + _rms(o_ref[...]) * g_ref[...]


def _merge(x, oa, ob, oc, gates, w_branch, w_out, l, g, *, tm=512, tc=512):
    t, d = x.shape
    tm = min(tm, t)
    nc = d // tc
    bw = oa.shape[1]
    vmem = (2 * tm * d * 4 + 2 * tm * d * 4 + 2 * 3 * tm * bw * 2 + 2 * 3 * tm * tc * 2 + 2 * 3 * bw * tc * 2
            + 2 * tc * d * 2 + 6 * tm * tc * 4 + tm * d * 4)
    o_spec = pl.BlockSpec((tm, bw), lambda i, c: (i, 0))
    gate_spec = lambda n: pl.BlockSpec((tm, tc), lambda i, c: (i, n * nc + c))
    wb_spec = lambda n: pl.BlockSpec((None, bw, tc), lambda i, c: (N_BRANCH * l + n, 0, c))
    return pl.pallas_call(
        functools.partial(_merge_body, nc=nc),
        grid=(t // tm, nc),
        in_specs=[pl.BlockSpec((tm, d), lambda i, c: (i, 0)), o_spec, o_spec, o_spec,
                  gate_spec(0), gate_spec(1), gate_spec(2), wb_spec(0), wb_spec(1), wb_spec(2),
                  pl.BlockSpec((None, tc, d), lambda i, c: (l, c, 0)),
                  pl.BlockSpec((1, d), lambda i, c: (0, 0))],
        out_specs=pl.BlockSpec((tm, d), lambda i, c: (i, 0)),
        out_shape=jax.ShapeDtypeStruct((t, d), F32),
        compiler_params=_cparams(("parallel", "arbitrary"), vmem),
        name="merge",
    )(x, oa, ob, oc, gates, gates, gates, w_branch, w_branch, w_branch, w_out, g.reshape(1, d))


def kernel(x, positions, norm_g, ffn_w_gate, ffn_w_up, ffn_w_down, w_in, dn_conv_w, dn_a_log, dn_dt_bias, dn_norm_g, sg_ln_g, sg_ln_b, sg_w, sg_b, mla_cq_norm_g, mla_ckv_norm_g, mla_w_uq, mla_w_ukv, w_branch, w_out):
    b, s, d = x.shape
    t = b * s
    xf = x.reshape(t, d)
    cs = _rope_table(positions)
    wb_all, wo_all = _cast_bf16(w_branch), _cast_bf16(w_out)
    ffn_f32 = tuple(w.reshape((2 * DEPTH,) + w.shape[2:]) for w in (ffn_w_gate, ffn_w_up, ffn_w_down))
    ffn_bf16 = [_cast_bf16(w, only=0) for w in ffn_f32]
    for l in range(DEPTH):
        ng = norm_g[l]
        xf, *nxt = _ffn(xf, ng[0], *ffn_bf16, 0, ng[1], cast_next=ffn_f32 + (2 * l + 1,))
        ffn_bf16 = [w[None] for w in nxt]
        proj, gates = _norm_proj(xf, ng[2], _relayout_w_in(w_in, l))
        proj3 = proj.reshape(b, s, ACT_COLS)
        o_a = _deltanet(proj3, dn_conv_w[l], dn_a_log[l], dn_dt_bias[l], dn_norm_g[l])
        o_b = _gmlp(proj3, sg_ln_g[l], sg_ln_b[l], sg_w[l], sg_b[l])
        w_uk, w_uvt = _split_w_ukv(mla_w_ukv[l])
        q, k, vt = _mla_proj(proj3, cs.reshape(b, s, LANES), mla_cq_norm_g[l], mla_ckv_norm_g[l],
                             _relayout_w_uq(mla_w_uq[l]), w_uk, w_uvt)
        o_c = _attention(q, k, vt)
        xf = _merge(xf, o_a.reshape(t, -1), o_b.reshape(t, -1), o_c.reshape(t, -1), gates, wb_all, wo_all, l, ng[3])
        xf, *nxt = _ffn(xf, ng[4], *ffn_bf16, 0, ng[5],
                        cast_next=ffn_f32 + (2 * l + 2,) if l + 1 < DEPTH else None)
        ffn_bf16 = [w[None] for w in nxt]
    return xf.reshape(b, s, d)
```

```python
import functools

import jax
import jax.numpy as jnp
from jax import lax
from jax.experimental import pallas as pl
from jax.experimental.pallas import tpu as pltpu

F32 = jnp.float32
BF16 = jnp.bfloat16

D_MODEL = 2048
DEPTH = 2
CHUNK = 64
NORM_EPS = 1e-6
DN_HEADS = 8
DN_HEAD_DIM = 128
DN_WIDTH = DN_HEADS * DN_HEAD_DIM
DN_CONV = 4
SG_GROUPS = 8
SG_GROUP_DIM = 128
SG_WIDTH = SG_GROUPS * SG_GROUP_DIM
SG_BLOCK = 128
MLA_HEADS = 8
MLA_Q_RANK = 512
MLA_KV_RANK = 512
MLA_NOPE = 128
MLA_ROPE = 64
MLA_V = 128
ROPE_THETA = 10000.0
N_BRANCH = 3
BRANCH_WIDTH = 1024
D_FF = 5632

LANES = 128
SUBLANES = 8
V7X_VMEM_BYTES = 64 * 1024 * 1024
VMEM_CAP = V7X_VMEM_BYTES - 6 * 1024 * 1024

OFF_QKV = 0
OFF_Z = OFF_QKV + 3 * DN_WIDTH
OFF_SG = OFF_Z + DN_WIDTH
OFF_GATE = OFF_SG + 2 * SG_WIDTH
OFF_CQ = OFF_GATE + N_BRANCH * D_MODEL
OFF_CKV = OFF_CQ + MLA_Q_RANK
OFF_AB = OFF_CKV + MLA_KV_RANK
OFF_KR = OFF_AB + LANES
PROJ_TILE = 512
PROJ_COLS = -(-(OFF_KR + LANES) // PROJ_TILE) * PROJ_TILE
PROJ_TAIL = PROJ_COLS - OFF_AB
MLA_QW = 2 * LANES

SRC_AB = 4 * DN_WIDTH
SRC_SG = SRC_AB + 2 * DN_HEADS
SRC_CQ = SRC_SG + 2 * SG_WIDTH
SRC_CKV = SRC_CQ + MLA_Q_RANK
SRC_KR = SRC_CKV + MLA_KV_RANK
SRC_GATE = SRC_KR + MLA_ROPE
IN_COLS = SRC_GATE + N_BRANCH * D_MODEL


def _cparams(semantics, vmem_bytes):
    return pltpu.CompilerParams(dimension_semantics=semantics,
                                vmem_limit_bytes=int(min(max(vmem_bytes, 16 * 1024 * 1024), VMEM_CAP)))


def _dot(a, b):
    return jnp.dot(a, b, preferred_element_type=F32)


def _dot_nt(a, b):
    return lax.dot_general(a, b, (((1,), (1,)), ((), ())), preferred_element_type=F32)


def _dot_tn(a, b):
    return lax.dot_general(a, b, (((0,), (0,)), ((), ())), preferred_element_type=F32)


def _rms(x, eps=NORM_EPS):
    return x * lax.rsqrt(jnp.mean(x * x, axis=-1, keepdims=True) + eps)


def _silu(x):
    return x * jax.nn.sigmoid(x)


def _cast_body(w_ref, o_ref):
    o_ref[...] = w_ref[...].astype(o_ref.dtype)


def _cast_bf16(w, only=None, *, block_bytes=4 * 1024 * 1024):
    r, c = w.shape[-2:]
    w = w.reshape(-1, r, c)
    n, first = (w.shape[0], 0) if only is None else (1, only)
    rb = r
    while rb * c * 4 > block_bytes and rb % (4 * SUBLANES) == 0:
        rb //= 2
    return pl.pallas_call(
        _cast_body,
        grid=(n, r // rb),
        in_specs=[pl.BlockSpec((1, rb, c), lambda m, i: (first + m, i, 0))],
        out_specs=pl.BlockSpec((1, rb, c), lambda m, i: (m, i, 0)),
        out_shape=jax.ShapeDtypeStruct((n, r, c), BF16),
        compiler_params=_cparams(("parallel", "parallel"), 3 * 2 * rb * c * 4),
        name="cast_bf16",
    )(w)


_WIDE_SEGMENTS = ((OFF_QKV, 0, SRC_AB), (OFF_SG, SRC_SG, 2 * SG_WIDTH), (OFF_GATE, SRC_GATE, N_BRANCH * D_MODEL),
                  (OFF_CQ, SRC_CQ, MLA_Q_RANK), (OFF_CKV, SRC_CKV, MLA_KV_RANK))


RELAYOUT_BLOCK = PROJ_TAIL
RELAYOUT_WIDE_BLOCKS = OFF_AB // RELAYOUT_BLOCK
RELAYOUT_BLOCKS = PROJ_COLS // RELAYOUT_BLOCK


def _relayout_step(step, wt_ref, tail_ref, o_ref):
    @pl.when(step < RELAYOUT_WIDE_BLOCKS)
    def _():
        o_ref[...] = wt_ref[0].T.astype(BF16)

    @pl.when(step == RELAYOUT_WIDE_BLOCKS)
    def _():
        o_ref[...] = tail_ref[...].astype(BF16)


def _relayout_src_start(blk):
    dst = blk * RELAYOUT_BLOCK
    start = 0
    for d0, s0, width in _WIDE_SEGMENTS:
        start = jnp.where((dst >= d0) & (dst < d0 + width), s0 + dst - d0, start)
    return pl.multiple_of(start, 2 * SUBLANES)


def _swap_halves(w):
    half = w.shape[-1] // 2
    return jnp.concatenate([-w[..., half:], w[..., :half]], axis=-1)


def _w_in_tail(w_in, l):
    d = w_in.shape[1]
    kr = w_in[l, :, SRC_KR:SRC_GATE]
    return jnp.concatenate([w_in[l, :, SRC_AB:SRC_SG], jnp.zeros((d, LANES - 2 * DN_HEADS), F32), kr, _swap_halves(kr),
                            jnp.zeros((d, PROJ_TAIL - 2 * LANES), F32)], axis=1)


def _relayout_w_uq(w):
    r = w.shape[0]
    w = w.reshape(r, MLA_HEADS, MLA_NOPE + MLA_ROPE)
    rope = w[..., MLA_NOPE:]
    return jnp.concatenate([w[..., :MLA_NOPE], rope, _swap_halves(rope)], axis=-1).reshape(r, MLA_HEADS * MLA_QW).astype(BF16)


def _split_w_ukv(w):
    r = w.shape[0]
    w = w.reshape(r, MLA_HEADS, MLA_NOPE + MLA_V)
    w_uk = w[..., :MLA_NOPE].reshape(r, MLA_HEADS * MLA_NOPE).astype(BF16)
    w_uvt = w[..., MLA_NOPE:].reshape(r, MLA_HEADS * MLA_V).T.astype(BF16)
    return w_uk, w_uvt


def _ffn_body(x_ref, gpre_ref, wg_ref, wu_ref, wd_ref, gpost_ref, *rest, nf, n_cast, relayout):
    n_in = n_cast + (2 if relayout else 0)
    cast_src, o_ref, cast_dst, xn_ref = rest[:n_cast], rest[n_in], rest[n_in + 1:n_in + 1 + n_cast], rest[-1]
    f = pl.program_id(1)

    @pl.when(f == 0)
    def _():
        xn_ref[...] = (_rms(x_ref[...]) * gpre_ref[...]).astype(BF16)
        o_ref[...] = jnp.zeros_like(o_ref)

    xn = xn_ref[...]
    h = _dot(xn, wg_ref[...])
    u = _dot(xn, wu_ref[...])
    a = (_silu(h) * u).astype(BF16)
    o_ref[...] += _dot(a, wd_ref[...])
    for src, dst in zip(cast_src, cast_dst):
        dst[...] = src[...].astype(dst.dtype)
    if relayout:
        _relayout_step(pl.program_id(0) * nf + f, rest[n_cast], rest[n_cast + 1], rest[n_in + 1 + n_cast])

    @pl.when(f == nf - 1)
    def _():
        o_ref[...] = x_ref[...] + 0.5 * (_rms(o_ref[...]) * gpost_ref[...])


def _ffn(x, g_pre, w_gate, w_up, w_down, wi, g_post, cast_next=None, relayout_w_in=None, *, tm=512, tf=512):
    t, d = x.shape
    tm = min(tm, t)
    dff = w_gate.shape[-1]
    nf = dff // tf
    nt = t // tm
    vmem = (2 * tm * d * 4 + 2 * tm * d * 4 + tm * d * 2 + 2 * 3 * d * tf * 2 + 5 * tm * tf * 4 + tm * d * 4)
    in_specs = [
        pl.BlockSpec((tm, d), lambda i, f: (i, 0)),
        pl.BlockSpec((1, d), lambda i, f: (0, 0)),
        pl.BlockSpec((None, d, tf), lambda i, f: (wi, 0, f)),
        pl.BlockSpec((None, d, tf), lambda i, f: (wi, 0, f)),
        pl.BlockSpec((None, tf, d), lambda i, f: (wi, f, 0)),
        pl.BlockSpec((1, d), lambda i, f: (0, 0)),
    ]
    out_specs = [pl.BlockSpec((tm, d), lambda i, f: (i, 0))]
    out_shape = [jax.ShapeDtypeStruct((t, d), F32)]
    operands = [x, g_pre.reshape(1, d), w_gate, w_up, w_down, g_post.reshape(1, d)]
    n_cast = 0
    if cast_next is not None:
        *srcs, ci = cast_next
        n_cast = len(srcs)
        dr = d // nt
        in_specs += [pl.BlockSpec((None, dr, tf), lambda i, f: (ci, i, f)),
                     pl.BlockSpec((None, dr, tf), lambda i, f: (ci, i, f)),
                     pl.BlockSpec((None, tf, dr), lambda i, f: (ci, f, i))]
        out_specs += [pl.BlockSpec((dr, tf), lambda i, f: (i, f)), pl.BlockSpec((dr, tf), lambda i, f: (i, f)),
                      pl.BlockSpec((tf, dr), lambda i, f: (f, i))]
        out_shape += [jax.ShapeDtypeStruct(s.shape[-2:], BF16) for s in srcs]
        operands += srcs
        vmem += 2 * 3 * dr * tf * (4 + 2)
    if relayout_w_in is not None:
        w_in, l = relayout_w_in
        assert nt * nf >= RELAYOUT_BLOCKS
        tb = RELAYOUT_BLOCK
        blk = lambda i, f: jnp.minimum(i * nf + f, RELAYOUT_BLOCKS - 1)
        in_specs += [pl.BlockSpec((pl.Element(1), pl.Element(tb), pl.Element(d)),
                                  lambda i, f: (l, _relayout_src_start(blk(i, f)), 0)),
                     pl.BlockSpec((d, tb), lambda i, f: (0, 0))]
        out_specs += [pl.BlockSpec((d, tb), lambda i, f: (0, blk(i, f)))]
        out_shape += [jax.ShapeDtypeStruct((d, PROJ_COLS), BF16)]
        operands += [jnp.swapaxes(w_in, 1, 2), _w_in_tail(w_in, l)]
        vmem += 2 * 2 * tb * d * 4 + 2 * tb * d * 2 + tb * d * 4
    semantics = ("parallel", "arbitrary") if relayout_w_in is None else ("arbitrary", "arbitrary")
    return pl.pallas_call(
        functools.partial(_ffn_body, nf=nf, n_cast=n_cast, relayout=relayout_w_in is not None),
        grid=(nt, nf),
        in_specs=in_specs,
        out_specs=out_specs,
        out_shape=out_shape,
        scratch_shapes=[pltpu.VMEM((tm, d), BF16)],
        compiler_params=_cparams(semantics, vmem),
        name="ffn",
    )(*operands)


def _proj_body(x_ref, g_ref, w_ref, o_ref, xn_ref):
    @pl.when(pl.program_id(1) == 0)
    def _():
        xn_ref[...] = (_rms(x_ref[...]) * g_ref[...]).astype(BF16)

    o_ref[...] = _dot(xn_ref[...], w_ref[...])


def _norm_proj(x, g, w, *, tm=1024, tn=3 * PROJ_TILE):
    t, d = x.shape
    n = w.shape[1]
    tm = min(tm, t)
    vmem = tm * d * 4 + tm * d * 2 + 2 * d * tn * 2 + 3 * tm * tn * 4 + tm * d * 4
    return pl.pallas_call(
        _proj_body,
        grid=(t // tm, n // tn),
        in_specs=[
            pl.BlockSpec((tm, d), lambda i, j: (i, 0), pipeline_mode=pl.Buffered(1)),
            pl.BlockSpec((1, d), lambda i, j: (0, 0)),
            pl.BlockSpec((d, tn), lambda i, j: (0, j)),
        ],
        out_specs=pl.BlockSpec((tm, tn), lambda i, j: (i, j)),
        out_shape=jax.ShapeDtypeStruct((t, n), F32),
        scratch_shapes=[pltpu.VMEM((tm, d), BF16)],
        compiler_params=_cparams(("parallel", "arbitrary"), vmem),
        name="norm_proj",
    )(x, g.reshape(1, d), w)


def _rope_body(pos_ref, inv_ref, o_ref):
    ang = pos_ref[...] * inv_ref[...]
    lane = lax.broadcasted_iota(jnp.int32, ang.shape, 1)
    o_ref[...] = jnp.where(lane < 2 * (MLA_ROPE // 2), jnp.cos(ang), jnp.sin(ang))


def _rope_table(positions):
    b, s = positions.shape
    t = b * s
    inv_freq = jnp.power(ROPE_THETA, -jnp.arange(0, MLA_ROPE, 2, dtype=F32) / MLA_ROPE)
    inv = jnp.tile(inv_freq, LANES // (MLA_ROPE // 2)).reshape(1, LANES)
    pos = jnp.broadcast_to(positions.astype(F32).reshape(t, 1), (t, LANES))
    tm = min(1024, t)
    return pl.pallas_call(
        _rope_body,
        grid=(t // tm,),
        in_specs=[pl.BlockSpec((tm, LANES), lambda i: (i, 0)), pl.BlockSpec((1, LANES), lambda i: (0, 0))],
        out_specs=pl.BlockSpec((tm, LANES), lambda i: (i, 0)),
        out_shape=jax.ShapeDtypeStruct((t, LANES), F32),
        compiler_params=_cparams(("parallel",), 0),
        name="rope_table",
    )(pos, inv)


DN_HALO = SUBLANES


def _dn_body(alog_ref, dtb_ref, q_ref, k_ref, v_ref, z_ref, ab_ref, wq_ref, wk_ref, wv_ref, ng_ref, o_ref,
             state_ref, halo_ref):
    sblk = q_ref.shape[1]
    hd = DN_HEAD_DIM

    @pl.when(pl.program_id(1) == 0)
    def _():
        state_ref[...] = jnp.zeros_like(state_ref)
        halo_ref[...] = jnp.zeros_like(halo_ref)

    lane = lax.broadcasted_iota(jnp.int32, (1, LANES), 1)
    row = lax.broadcasted_iota(jnp.int32, (CHUNK, CHUNK), 0)
    col = lax.broadcasted_iota(jnp.int32, (CHUNK, CHUNK), 1)
    tril = (row >= col).astype(F32)
    row2 = lax.broadcasted_iota(jnp.int32, (2 * CHUNK, 2 * CHUNK), 0)
    col2i = lax.broadcasted_iota(jnp.int32, (2 * CHUNK, 2 * CHUNK), 1)
    same_head = (row2 // CHUNK) == (col2i // CHUNK)
    incl2 = same_head & (row2 >= col2i)
    strict2 = same_head & (row2 > col2i)
    a_scale = jnp.where(lane < DN_HEADS, -jnp.exp(alog_ref[...]), 0.0)
    dt_bias = dtb_ref[...]
    ng = ng_ref[...]
    seq_refs = (q_ref, k_ref, v_ref)
    w_refs = (wq_ref, wk_ref, wv_ref)

    def conv_silu(which, sl, c, c0):
        ref = seq_refs[which]
        cur = ref[0, pl.ds(c0, CHUNK), sl]
        p0 = pl.multiple_of(jnp.maximum(c0 - DN_HALO, 0), DN_HALO)
        prev = jnp.where(c > 0, ref[0, pl.ds(p0, DN_HALO), sl], halo_ref[which, :, sl])
        xx = jnp.concatenate([prev, cur], axis=0)
        w = w_refs[which][:, sl]
        first = DN_HALO - (DN_CONV - 1)
        y = w[0:1] * xx[first:first + CHUNK]
        for j in range(1, DN_CONV):
            y = y + w[j:j + 1] * xx[first + j:first + j + CHUNK]
        return _silu(y)

    heads = range(DN_HEADS)
    pairs = range(DN_HEADS // 2)
    unroll = 4
    items = [(ci, pr) for ci in range(unroll) for pr in pairs]
    sls = [slice(h * hd, (h + 1) * hd) for h in heads]
    top, bot = slice(0, CHUNK), slice(CHUNK, 2 * CHUNK)
    half = lambda h: top if h % 2 == 0 else bot

    def col2(a, off, pr):
        return jnp.concatenate([a[:, off + 2 * pr:off + 2 * pr + 1], a[:, off + 2 * pr + 1:off + 2 * pr + 2]], axis=0)

    def gate_terms(c0):
        ab = ab_ref[0, pl.ds(c0, CHUNK), :]
        xa = ab + dt_bias
        g_all = a_scale * (jnp.maximum(xa, 0.0) + jnp.log1p(jnp.exp(-jnp.abs(xa))))
        beta_all = jax.nn.sigmoid(ab)
        gc_all = jnp.dot(tril, g_all, precision=lax.Precision.HIGHEST, preferred_element_type=F32)
        g_last = gc_all[CHUNK - 1:CHUNK, :]
        gc_t = jnp.concatenate([gc_all, gc_all], axis=0).T
        return dict(beta=beta_all, gc=gc_all, gc_t=gc_t, eg=jnp.exp(gc_all), ed=jnp.exp(g_last - gc_all),
                    egl=jnp.exp(g_last))

    def chunk_group_step(it, carry):
        cidx = [it * unroll + ci for ci in range(unroll)]
        c0s = [pl.multiple_of(c * CHUNK, CHUNK) for c in cidx]
        gt = [gate_terms(c0) for c0 in c0s]

        def stacked(fn, ci, pr):
            return jnp.concatenate([fn(ci, 2 * pr), fn(ci, 2 * pr + 1)], axis=0)

        z2 = {(ci, pr): stacked(lambda ci, h: z_ref[0, pl.ds(c0s[ci], CHUNK), sls[h]], ci, pr) for ci, pr in items}
        q2 = {(ci, pr): stacked(lambda ci, h: conv_silu(0, sls[h], cidx[ci], c0s[ci]), ci, pr) for ci, pr in items}
        k2 = {(ci, pr): stacked(lambda ci, h: conv_silu(1, sls[h], cidx[ci], c0s[ci]), ci, pr) for ci, pr in items}
        v2 = {(ci, pr): stacked(lambda ci, h: conv_silu(2, sls[h], cidx[ci], c0s[ci]), ci, pr) for ci, pr in items}
        q2 = {i: x * lax.rsqrt(jnp.sum(x * x, axis=-1, keepdims=True) + 1e-6) * (hd ** -0.5) for i, x in q2.items()}
        k2 = {i: x * lax.rsqrt(jnp.sum(x * x, axis=-1, keepdims=True) + 1e-6) for i, x in k2.items()}
        beta2 = {(ci, pr): col2(gt[ci]["beta"], DN_HEADS, pr) for ci, pr in items}
        eg2 = {(ci, pr): col2(gt[ci]["eg"], 0, pr) for ci, pr in items}
        g_row2 = {(ci, pr): jnp.where(lane < CHUNK, gt[ci]["gc_t"][2 * pr:2 * pr + 1], gt[ci]["gc_t"][2 * pr + 1:2 * pr + 2])
                  for ci, pr in items}
        decay2 = {(ci, pr): jnp.exp(jnp.where(incl2, col2(gt[ci]["gc"], 0, pr) - g_row2[ci, pr], -jnp.inf))
                  for ci, pr in items}
        kb2 = {i: k2[i] * beta2[i] for i in items}
        km2 = {i: k2[i].astype(BF16) for i in items}
        kk2 = {i: _dot_nt(jnp.concatenate([kb2[i], q2[i]], axis=0).astype(BF16), km2[i]) for i in items}
        qk2 = {i: (kk2[i][2 * CHUNK:] * decay2[i]).astype(BF16) for i in items}
        p2 = {i: jnp.where(strict2, -(kk2[i][:2 * CHUNK] * decay2[i]), 0.0) for i in items}
        sol2 = {i: jnp.concatenate([kb2[i] * eg2[i], v2[i] * beta2[i]], axis=1) for i in items}
        sol2 = {i: sol2[i] + _dot(p2[i].astype(BF16), sol2[i].astype(BF16)) for i in items}
        for _ in range(5):
            pb = {i: p2[i].astype(BF16) for i in items}
            p2 = {i: _dot(pb[i], pb[i]) for i in items}
            sol2 = {i: sol2[i] + _dot(p2[i].astype(BF16), sol2[i].astype(BF16)) for i in items}
        wq2 = {(ci, h): jnp.concatenate([sol2[ci, h // 2][half(h), :hd], (q2[ci, h // 2] * eg2[ci, h // 2])[half(h)]],
                                        axis=0).astype(BF16) for ci in range(unroll) for h in heads}
        kd2 = {(ci, pr): (k2[ci, pr] * col2(gt[ci]["ed"], 0, pr)).astype(BF16) for ci, pr in items}
        states = [state_ref[h] for h in heads]
        for ci in range(unroll):
            ws = [_dot(wq2[ci, h], states[h].astype(BF16)) for h in heads]
            vn2 = [(sol2[ci, pr][:, hd:] - jnp.concatenate([ws[2 * pr][:CHUNK], ws[2 * pr + 1][:CHUNK]], axis=0)).astype(BF16)
                   for pr in pairs]
            states = [states[h] * gt[ci]["egl"][:, h:h + 1] + _dot_tn(kd2[ci, h // 2][half(h)], vn2[h // 2][half(h)])
                      for h in heads]
            o2 = [jnp.concatenate([ws[2 * pr][CHUNK:], ws[2 * pr + 1][CHUNK:]], axis=0) + _dot(qk2[ci, pr], vn2[pr])
                  for pr in pairs]
            out2 = [(_rms(o2[pr]) * ng * _silu(z2[ci, pr])).astype(o_ref.dtype) for pr in pairs]
            for h in heads:
                o_ref[0, pl.ds(c0s[ci], CHUNK), sls[h]] = out2[h // 2][half(h)]
        for h in heads:
            state_ref[h] = states[h]
        return carry

    lax.fori_loop(0, sblk // (CHUNK * unroll), chunk_group_step, 0)
    for which in range(3):
        halo_ref[which] = seq_refs[which][0, sblk - DN_HALO:sblk, :]


def _deltanet(proj3, conv_w, a_log, dt_bias, norm_g, *, sblk=512):
    b, s, _ = proj3.shape
    sblk = min(sblk, s)
    wd = DN_WIDTH
    pad = lambda vec: jnp.concatenate([vec, jnp.zeros((LANES - DN_HEADS,), F32)]).reshape(1, LANES)
    seq_spec = lambda blk: pl.BlockSpec((1, sblk, wd), lambda i, j: (i, j, blk))
    w_spec = lambda blk: pl.BlockSpec((DN_CONV, wd), lambda i, j: (0, blk))
    vec_spec = pl.BlockSpec((1, LANES), lambda i, j: (0, 0))
    live_chunk_values = 16 * 1024 * 1024
    vmem = 2 * 4 * sblk * wd * 4 + 2 * sblk * wd * 2 + 2 * sblk * LANES * 4 + live_chunk_values
    return pl.pallas_call(
        _dn_body,
        grid=(b, s // sblk),
        in_specs=[vec_spec, vec_spec,
                  seq_spec(OFF_QKV // wd), seq_spec(OFF_QKV // wd + 1), seq_spec(OFF_QKV // wd + 2), seq_spec(OFF_Z // wd),
                  pl.BlockSpec((1, sblk, LANES), lambda i, j: (i, j, OFF_AB // LANES)),
                  w_spec(0), w_spec(1), w_spec(2), vec_spec],
        out_specs=pl.BlockSpec((1, sblk, wd), lambda i, j: (i, j, 0)),
        out_shape=jax.ShapeDtypeStruct((b, s, wd), BF16),
        scratch_shapes=[pltpu.VMEM((DN_HEADS, DN_HEAD_DIM, DN_HEAD_DIM), F32), pltpu.VMEM((3, DN_HALO, wd), F32)],
        compiler_params=_cparams(("parallel", "arbitrary"), vmem),
        name="deltanet",
    )(pad(a_log), pad(dt_bias), proj3, proj3, proj3, proj3, proj3, conv_w, conv_w, conv_w, norm_g.reshape(1, -1))


def _gelu(x):
    return 0.5 * x * (1.0 + lax.erf(x * (0.5 ** 0.5)))


def _sg_body(u_ref, v_ref, lng_ref, lnb_ref, w_ref, bt_ref, o_ref):
    u = _gelu(u_ref[0])
    v = _gelu(v_ref[0])
    mu = jnp.mean(v, axis=-1, keepdims=True)
    vc = v - mu
    var = jnp.mean(vc * vc, axis=-1, keepdims=True)
    vn = (vc * lax.rsqrt(var + 1e-5) * lng_ref[...] + lnb_ref[...]).astype(BF16)
    row = lax.broadcasted_iota(jnp.int32, (SG_BLOCK, SG_BLOCK), 0)
    col = lax.broadcasted_iota(jnp.int32, (SG_BLOCK, SG_BLOCK), 1)
    mask = (col // CHUNK) <= (row // CHUNK)
    for g in range(SG_GROUPS):
        sl = slice(g * SG_GROUP_DIM, (g + 1) * SG_GROUP_DIM)
        wg = jnp.where(mask, w_ref[g], 0.0).astype(BF16)
        mixed = _dot(wg, vn[:, sl]) + bt_ref[:, g:g + 1]
        o_ref[0, :, sl] = (u[:, sl] * mixed).astype(o_ref.dtype)


def _gmlp(proj3, ln_g, ln_b, sg_w, sg_b):
    b, s, _ = proj3.shape
    ub = OFF_SG // SG_WIDTH
    vmem = 2 * 2 * SG_BLOCK * SG_WIDTH * 4 + 2 * SG_GROUPS * SG_BLOCK * SG_BLOCK * 4 + 8 * SG_BLOCK * SG_WIDTH * 4
    return pl.pallas_call(
        _sg_body,
        grid=(b, s // SG_BLOCK),
        in_specs=[
            pl.BlockSpec((1, SG_BLOCK, SG_WIDTH), lambda i, n: (i, n, ub)),
            pl.BlockSpec((1, SG_BLOCK, SG_WIDTH), lambda i, n: (i, n, ub + 1)),
            pl.BlockSpec((1, SG_WIDTH), lambda i, n: (0, 0)),
            pl.BlockSpec((1, SG_WIDTH), lambda i, n: (0, 0)),
            pl.BlockSpec((SG_GROUPS, SG_BLOCK, SG_BLOCK), lambda i, n: (0, 0, 0)),
            pl.BlockSpec((SG_BLOCK, SG_GROUPS), lambda i, n: (0, 0)),
        ],
        out_specs=pl.BlockSpec((1, SG_BLOCK, SG_WIDTH), lambda i, n: (i, n, 0)),
        out_shape=jax.ShapeDtypeStruct((b, s, SG_WIDTH), BF16),
        compiler_params=_cparams(("parallel", "parallel"), vmem),
        name="gmlp",
    )(proj3, proj3, ln_g.reshape(1, SG_WIDTH), ln_b.reshape(1, SG_WIDTH), sg_w, sg_b.T)


def _rotate(block, cs):
    t = block * cs
    return t + pltpu.roll(t, MLA_ROPE, 1)


def _mla_proj_body(cq_ref, ckv_ref, kr_ref, cs_ref, gq_ref, gkv_ref, wq_ref, wk_ref, wvt_ref, q_out, k_out, vt_out):
    cs = cs_ref[0]
    scale = (MLA_NOPE + MLA_ROPE) ** -0.5
    cqn = (_rms(cq_ref[0]) * gq_ref[...]).astype(BF16)
    q = _dot(cqn, wq_ref[...])
    lane = lax.broadcasted_iota(jnp.int32, cs.shape, 1)
    k_rope = jnp.where(lane < MLA_ROPE, _rotate(kr_ref[0], cs), 0.0).astype(k_out.dtype)
    ckvn = (_rms(ckv_ref[0]) * gkv_ref[...]).astype(BF16)
    k = _dot(ckvn, wk_ref[...])
    for h in range(MLA_HEADS):
        base = h * MLA_QW
        q_out[0, :, base:base + MLA_NOPE] = (q[:, base:base + MLA_NOPE] * scale).astype(q_out.dtype)
        q_out[0, :, base + MLA_NOPE:base + MLA_QW] = (_rotate(q[:, base + MLA_NOPE:base + MLA_QW], cs) * scale).astype(q_out.dtype)
        k_out[0, :, base:base + MLA_NOPE] = k[:, h * MLA_NOPE:(h + 1) * MLA_NOPE].astype(k_out.dtype)
        k_out[0, :, base + MLA_NOPE:base + MLA_QW] = k_rope
    vt_out[0] = _dot_nt(wvt_ref[...], ckvn).astype(vt_out.dtype)


def _mla_proj(proj3, cs3, cq_g, ckv_g, w_uq, w_uk, w_uvt, *, tm=512):
    b, s, _ = proj3.shape
    tm = min(tm, s)
    nq = w_uq.shape[1]
    nk = MLA_HEADS * MLA_QW
    nv = w_uvt.shape[0]
    vmem = (2 * tm * (MLA_Q_RANK + MLA_KV_RANK + 2 * LANES) * 4 + 2 * MLA_Q_RANK * (nq + w_uk.shape[1] + nv) * 2
            + 2 * tm * (nq + nk + nv) * 2 + 3 * tm * (nq + nk + nv) * 4)
    const = lambda shape: pl.BlockSpec(shape, lambda bi, i: (0, 0))
    return pl.pallas_call(
        _mla_proj_body,
        grid=(b, s // tm),
        in_specs=[
            pl.BlockSpec((1, tm, MLA_Q_RANK), lambda bi, i: (bi, i, OFF_CQ // MLA_Q_RANK)),
            pl.BlockSpec((1, tm, MLA_KV_RANK), lambda bi, i: (bi, i, OFF_CKV // MLA_KV_RANK)),
            pl.BlockSpec((1, tm, LANES), lambda bi, i: (bi, i, OFF_KR // LANES)),
            pl.BlockSpec((1, tm, LANES), lambda bi, i: (bi, i, 0)),
            const((1, MLA_Q_RANK)), const((1, MLA_KV_RANK)),
            const(w_uq.shape), const(w_uk.shape), const(w_uvt.shape),
        ],
        out_specs=[
            pl.BlockSpec((1, tm, nq), lambda bi, i: (bi, i, 0)),
            pl.BlockSpec((1, tm, nk), lambda bi, i: (bi, i, 0)),
            pl.BlockSpec((1, nv, tm), lambda bi, i: (bi, 0, i)),
        ],
        out_shape=[jax.ShapeDtypeStruct((b, s, nq), BF16), jax.ShapeDtypeStruct((b, s, nk), BF16),
                   jax.ShapeDtypeStruct((b, nv, s), BF16)],
        compiler_params=_cparams(("parallel", "parallel"), vmem),
        name="mla_proj",
    )(proj3, proj3, proj3, cs3, cq_g.reshape(1, -1), ckv_g.reshape(1, -1), w_uq, w_uk, w_uvt)


ATTN_HEADS_PER_STEP = 8


def _attn_body(q_ref, k_ref, vt_ref, o_ref, *, tq, nh):
    i = pl.program_id(2)
    diag_mask = (lax.broadcasted_iota(jnp.int32, (tq, tq), 0) // CHUNK
                 <= lax.broadcasted_iota(jnp.int32, (tq, tq), 1) // CHUNK)

    def kv_step(j, carry, masked):
        k0 = pl.multiple_of(j * tq, tq)
        heads = range(nh)
        st = [_dot_nt(k_ref[0, pl.ds(k0, tq), h * MLA_QW:(h + 1) * MLA_QW], q_ref[0, :, h * MLA_QW:(h + 1) * MLA_QW])
              for h in heads]
        if masked:
            st = [jnp.where(diag_mask, x, -1e30) for x in st]
        m_new = [jnp.maximum(carry[h][0], jnp.max(st[h], axis=0, keepdims=True)) for h in heads]
        alpha = [jnp.exp(carry[h][0] - m_new[h]) for h in heads]
        pt = [jnp.exp(st[h] - m_new[h]) for h in heads]
        l = [alpha[h] * carry[h][1] + jnp.sum(pt[h], axis=0, keepdims=True) for h in heads]
        pv = [_dot(vt_ref[0, h * MLA_V:(h + 1) * MLA_V, pl.ds(k0, tq)], pt[h].astype(BF16)) for h in heads]
        return tuple((m_new[h], l[h], alpha[h] * carry[h][2] + pv[h]) for h in heads)

    init = tuple((jnp.full((1, tq), -jnp.inf, F32), jnp.zeros((1, tq), F32), jnp.zeros((MLA_V, tq), F32))
                 for _ in range(nh))
    carry = lax.fori_loop(0, i, functools.partial(kv_step, masked=False), init)
    carry = kv_step(i, carry, masked=True)
    for h in range(nh):
        _, l, acc = carry[h]
        o_ref[0, :, h * MLA_V:(h + 1) * MLA_V] = (acc / l).T.astype(o_ref.dtype)


def _attention(q3, k3, vt3, *, tq=256, nh=ATTN_HEADS_PER_STEP):
    b, s, _ = q3.shape
    tq = min(tq, s)
    vmem = (2 * tq * nh * MLA_QW * 2 + 2 * s * nh * MLA_QW * 2 + 2 * s * nh * MLA_V * 2 + 2 * tq * nh * MLA_V * 2
            + nh * 8 * tq * tq * 4)
    return pl.pallas_call(
        functools.partial(_attn_body, tq=tq, nh=nh),
        grid=(b, MLA_HEADS // nh, s // tq),
        in_specs=[
            pl.BlockSpec((1, tq, nh * MLA_QW), lambda bi, hg, i: (bi, i, hg)),
            pl.BlockSpec((1, s, nh * MLA_QW), lambda bi, hg, i: (bi, 0, hg)),
            pl.BlockSpec((1, nh * MLA_V, s), lambda bi, hg, i: (bi, hg, 0)),
        ],
        out_specs=pl.BlockSpec((1, tq, nh * MLA_V), lambda bi, hg, i: (bi, i, hg)),
        out_shape=jax.ShapeDtypeStruct((b, s, MLA_HEADS * MLA_V), BF16),
        compiler_params=_cparams(("parallel", "parallel", "arbitrary"), vmem),
        name="mla_attention",
    )(q3, k3, vt3)


def _merge_body(x_ref, oa_ref, ob_ref, oc_ref, ga_ref, gb_ref, gc_ref, wa_ref, wb_ref, wc_ref, wo_ref, g_ref,
                o_ref, *, nc):
    c = pl.program_id(1)

    @pl.when(c == 0)
    def _():
        o_ref[...] = jnp.zeros_like(o_ref)

    merged = jax.nn.sigmoid(ga_ref[...]) * _dot(oa_ref[...], wa_ref[...])
    merged = merged + jax.nn.sigmoid(gb_ref[...]) * _dot(ob_ref[...], wb_ref[...])
    merged = merged + jax.nn.sigmoid(gc_ref[...]) * _dot(oc_ref[...], wc_ref[...])
    o_ref[...] += _dot(merged.astype(BF16), wo_ref[...])

    @pl.when(c == nc - 1)
    def _():
        o_ref[...] = x_ref[...] + _rms(o_ref[...]) * g_ref[...]


def _merge(x, oa, ob, oc, proj, w_branch, w_out, l, g, *, tm=512, tc=512):
    t, d = x.shape
    tm = min(tm, t)
    nc = d // tc
    bw = oa.shape[1]
    vmem = (2 * tm * d * 4 + 2 * tm * d * 4 + 2 * 3 * tm * bw * 2 + 2 * 3 * tm * tc * 4 + 2 * 3 * bw * tc * 2
            + 2 * tc * d * 2 + 6 * tm * tc * 4 + tm * d * 4)
    o_spec = pl.BlockSpec((tm, bw), lambda i, c: (i, 0))
    gate_blk = OFF_GATE // tc
    gate_spec = lambda n: pl.BlockSpec((tm, tc), lambda i, c: (i, gate_blk + n * nc + c))
    wb_spec = lambda n: pl.BlockSpec((None, bw, tc), lambda i, c: (N_BRANCH * l + n, 0, c))
    return pl.pallas_call(
        functools.partial(_merge_body, nc=nc),
        grid=(t // tm, nc),
        in_specs=[pl.BlockSpec((tm, d), lambda i, c: (i, 0)), o_spec, o_spec, o_spec,
                  gate_spec(0), gate_spec(1), gate_spec(2), wb_spec(0), wb_spec(1), wb_spec(2),
                  pl.BlockSpec((None, tc, d), lambda i, c: (l, c, 0)),
                  pl.BlockSpec((1, d), lambda i, c: (0, 0))],
        out_specs=pl.BlockSpec((tm, d), lambda i, c: (i, 0)),
        out_shape=jax.ShapeDtypeStruct((t, d), F32),
        compiler_params=_cparams(("parallel", "arbitrary"), vmem),
        name="merge",
    )(x, oa, ob, oc, proj, proj, proj, w_branch, w_branch, w_branch, w_out, g.reshape(1, d))


def kernel(x, positions, norm_g, ffn_w_gate, ffn_w_up, ffn_w_down, w_in, dn_conv_w, dn_a_log, dn_dt_bias, dn_norm_g, sg_ln_g, sg_ln_b, sg_w, sg_b, mla_cq_norm_g, mla_ckv_norm_g, mla_w_uq, mla_w_ukv, w_branch, w_out):
    b, s, d = x.shape
    t = b * s
    xf = x.reshape(t, d)
    cs = _rope_table(positions)
    wb_all, wo_all = _cast_bf16(w_branch), _cast_bf16(w_out)
    ffn_f32 = tuple(w.reshape((2 * DEPTH,) + w.shape[2:]) for w in (ffn_w_gate, ffn_w_up, ffn_w_down))
    ffn_bf16 = [_cast_bf16(w, only=0) for w in ffn_f32]
    for l in range(DEPTH):
        ng = norm_g[l]
        xf, *nxt, w_in_bf16 = _ffn(xf, ng[0], *ffn_bf16, 0, ng[1], cast_next=ffn_f32 + (2 * l + 1,),
                                   relayout_w_in=(w_in, l))
        ffn_bf16 = [w[None] for w in nxt]
        proj = _norm_proj(xf, ng[2], w_in_bf16)
        proj3 = proj.reshape(b, s, PROJ_COLS)
        o_a = _deltanet(proj3, dn_conv_w[l], dn_a_log[l], dn_dt_bias[l], dn_norm_g[l])
        o_b = _gmlp(proj3, sg_ln_g[l], sg_ln_b[l], sg_w[l], sg_b[l])
        w_uk, w_uvt = _split_w_ukv(mla_w_ukv[l])
        q, k, vt = _mla_proj(proj3, cs.reshape(b, s, LANES), mla_cq_norm_g[l], mla_ckv_norm_g[l],
                             _relayout_w_uq(mla_w_uq[l]), w_uk, w_uvt)
        o_c = _attention(q, k, vt)
        xf = _merge(xf, o_a.reshape(t, -1), o_b.reshape(t, -1), o_c.reshape(t, -1), proj, wb_all, wo_all, l, ng[3])
        xf, *nxt = _ffn(xf, ng[4], *ffn_bf16, 0, ng[5],
                        cast_next=ffn_f32 + (2 * l + 2,) if l + 1 < DEPTH else None)
        ffn_bf16 = [w[None] for w in nxt]
    return xf.reshape(b, s, d)
```

```python
import functools

import jax
import jax.numpy as jnp
from jax import lax
from jax.experimental import pallas as pl
from jax.experimental.pallas import tpu as pltpu

F32 = jnp.float32
BF16 = jnp.bfloat16

D_MODEL = 2048
DEPTH = 2
CHUNK = 64
NORM_EPS = 1e-6
DN_HEADS = 8
DN_HEAD_DIM = 128
DN_WIDTH = DN_HEADS * DN_HEAD_DIM
DN_CONV = 4
SG_GROUPS = 8
SG_GROUP_DIM = 128
SG_WIDTH = SG_GROUPS * SG_GROUP_DIM
SG_BLOCK = 128
MLA_HEADS = 8
MLA_Q_RANK = 512
MLA_KV_RANK = 512
MLA_NOPE = 128
MLA_ROPE = 64
MLA_V = 128
ROPE_THETA = 10000.0
N_BRANCH = 3
BRANCH_WIDTH = 1024
D_FF = 5632

LANES = 128
SUBLANES = 8
V7X_VMEM_BYTES = 64 * 1024 * 1024
VMEM_CAP = V7X_VMEM_BYTES - 6 * 1024 * 1024

OFF_QKV = 0
OFF_Z = OFF_QKV + 3 * DN_WIDTH
OFF_SG = OFF_Z + DN_WIDTH
OFF_GATE = OFF_SG + 2 * SG_WIDTH
OFF_CQ = OFF_GATE + N_BRANCH * D_MODEL
OFF_CKV = OFF_CQ + MLA_Q_RANK
OFF_AB = OFF_CKV + MLA_KV_RANK
OFF_KR = OFF_AB + LANES
PROJ_TILE = 512
PROJ_COLS = -(-(OFF_KR + LANES) // PROJ_TILE) * PROJ_TILE
PROJ_TAIL = PROJ_COLS - OFF_AB
MLA_QW = 2 * LANES

SRC_AB = 4 * DN_WIDTH
SRC_SG = SRC_AB + 2 * DN_HEADS
SRC_CQ = SRC_SG + 2 * SG_WIDTH
SRC_CKV = SRC_CQ + MLA_Q_RANK
SRC_KR = SRC_CKV + MLA_KV_RANK
SRC_GATE = SRC_KR + MLA_ROPE
IN_COLS = SRC_GATE + N_BRANCH * D_MODEL


def _cparams(semantics, vmem_bytes):
    return pltpu.CompilerParams(dimension_semantics=semantics,
                                vmem_limit_bytes=int(min(max(vmem_bytes, 16 * 1024 * 1024), VMEM_CAP)))


def _dot(a, b):
    return jnp.dot(a, b, preferred_element_type=F32)


def _dot_nt(a, b):
    return lax.dot_general(a, b, (((1,), (1,)), ((), ())), preferred_element_type=F32)


def _dot_tn(a, b):
    return lax.dot_general(a, b, (((0,), (0,)), ((), ())), preferred_element_type=F32)


def _rms(x, eps=NORM_EPS):
    return x * lax.rsqrt(jnp.mean(x * x, axis=-1, keepdims=True) + eps)


def _silu(x):
    return x * jax.nn.sigmoid(x)


def _cast_body(w_ref, o_ref):
    o_ref[...] = w_ref[...].astype(o_ref.dtype)


def _cast_bf16(w, only=None, *, block_bytes=4 * 1024 * 1024):
    r, c = w.shape[-2:]
    w = w.reshape(-1, r, c)
    n, first = (w.shape[0], 0) if only is None else (1, only)
    rb = r
    while rb * c * 4 > block_bytes and rb % (4 * SUBLANES) == 0:
        rb //= 2
    return pl.pallas_call(
        _cast_body,
        grid=(n, r // rb),
        in_specs=[pl.BlockSpec((1, rb, c), lambda m, i: (first + m, i, 0))],
        out_specs=pl.BlockSpec((1, rb, c), lambda m, i: (m, i, 0)),
        out_shape=jax.ShapeDtypeStruct((n, r, c), BF16),
        compiler_params=_cparams(("parallel", "parallel"), 3 * 2 * rb * c * 4),
        name="cast_bf16",
    )(w)


_WIDE_SEGMENTS = ((OFF_QKV, 0, SRC_AB), (OFF_SG, SRC_SG, 2 * SG_WIDTH), (OFF_GATE, SRC_GATE, N_BRANCH * D_MODEL),
                  (OFF_CQ, SRC_CQ, MLA_Q_RANK), (OFF_CKV, SRC_CKV, MLA_KV_RANK))


def _cast_pair_body(a_ref, b_ref, o_ref):
    o_ref[0] = a_ref[0].astype(o_ref.dtype)
    o_ref[1] = b_ref[0].astype(o_ref.dtype)


def _cast_pair_bf16(a, b, k, *, block_bytes=2 * 1024 * 1024):
    n, r, c = a.shape
    rb = r
    while rb * c * 4 > block_bytes and rb % (4 * SUBLANES) == 0:
        rb //= 2
    src = pl.BlockSpec((1, rb, c), lambda i: (k, i, 0))
    return pl.pallas_call(
        _cast_pair_body,
        grid=(r // rb,),
        in_specs=[src, src],
        out_specs=pl.BlockSpec((2, rb, c), lambda i: (0, i, 0)),
        out_shape=jax.ShapeDtypeStruct((2, r, c), BF16),
        compiler_params=_cparams(("parallel",), 3 * 2 * 2 * rb * c * 4),
        name="cast_pair_bf16",
    )(a, b)


RELAYOUT_BLOCK = PROJ_TAIL
RELAYOUT_WIDE_BLOCKS = OFF_AB // RELAYOUT_BLOCK
RELAYOUT_BLOCKS = PROJ_COLS // RELAYOUT_BLOCK


def _relayout_step(step, wt_ref, tail_ref, o_ref):
    @pl.when(step < RELAYOUT_WIDE_BLOCKS)
    def _():
        o_ref[...] = wt_ref[0].T.astype(BF16)

    @pl.when(step == RELAYOUT_WIDE_BLOCKS)
    def _():
        o_ref[...] = tail_ref[...].astype(BF16)


def _relayout_src_start(blk):
    dst = blk * RELAYOUT_BLOCK
    start = 0
    for d0, s0, width in _WIDE_SEGMENTS:
        start = jnp.where((dst >= d0) & (dst < d0 + width), s0 + dst - d0, start)
    return pl.multiple_of(start, 2 * SUBLANES)


def _swap_halves(w):
    half = w.shape[-1] // 2
    return jnp.concatenate([-w[..., half:], w[..., :half]], axis=-1)


def _w_in_tail(w_in, l):
    d = w_in.shape[1]
    kr = w_in[l, :, SRC_KR:SRC_GATE]
    return jnp.concatenate([w_in[l, :, SRC_AB:SRC_SG], jnp.zeros((d, LANES - 2 * DN_HEADS), F32), kr, _swap_halves(kr),
                            jnp.zeros((d, PROJ_TAIL - 2 * LANES), F32)], axis=1)


def _relayout_w_uq(w):
    r = w.shape[0]
    w = w.reshape(r, MLA_HEADS, MLA_NOPE + MLA_ROPE)
    rope = w[..., MLA_NOPE:]
    return jnp.concatenate([w[..., :MLA_NOPE], rope, _swap_halves(rope)], axis=-1).reshape(r, MLA_HEADS * MLA_QW).astype(BF16)


def _split_w_ukv(w):
    r = w.shape[0]
    w = w.reshape(r, MLA_HEADS, MLA_NOPE + MLA_V)
    w_uk = w[..., :MLA_NOPE].reshape(r, MLA_HEADS * MLA_NOPE).astype(BF16)
    w_uvt = w[..., MLA_NOPE:].reshape(r, MLA_HEADS * MLA_V).T.astype(BF16)
    return w_uk, w_uvt


def _ffn_body(x_ref, gpre_ref, wg_ref, wu_ref, wd_ref, gpost_ref, *rest, nf, n_cast, relayout):
    n_in = n_cast + (2 if relayout else 0)
    cast_src, o_ref, cast_dst, xn_ref = rest[:n_cast], rest[n_in], rest[n_in + 1:n_in + 1 + n_cast], rest[-1]
    f = pl.program_id(1)

    @pl.when(f == 0)
    def _():
        xn_ref[...] = (_rms(x_ref[...]) * gpre_ref[...]).astype(BF16)
        o_ref[...] = jnp.zeros_like(o_ref)

    xn = xn_ref[...]
    h = _dot(xn, wg_ref[...])
    u = _dot(xn, wu_ref[...])
    a = (_silu(h) * u).astype(BF16)
    o_ref[...] += _dot(a, wd_ref[...])
    for src, dst in zip(cast_src, cast_dst):
        dst[...] = src[...].astype(dst.dtype)
    if relayout:
        _relayout_step(pl.program_id(0) * nf + f, rest[n_cast], rest[n_cast + 1], rest[n_in + 1 + n_cast])

    @pl.when(f == nf - 1)
    def _():
        o_ref[...] = x_ref[...] + 0.5 * (_rms(o_ref[...]) * gpost_ref[...])


def _ffn(x, g_pre, w_gate, w_up, w_down, wi, g_post, cast_next=None, relayout_w_in=None, *, tm=512, tf=512):
    t, d = x.shape
    tm = min(tm, t)
    dff = w_gate.shape[-1]
    nf = dff // tf
    nt = t // tm
    wi_gate, wi_up, wi_down = wi
    vmem = (2 * tm * d * 4 + 2 * tm * d * 4 + tm * d * 2 + 2 * 3 * d * tf * 2 + 5 * tm * tf * 4 + tm * d * 4)
    in_specs = [
        pl.BlockSpec((tm, d), lambda i, f: (i, 0)),
        pl.BlockSpec((1, d), lambda i, f: (0, 0)),
        pl.BlockSpec((None, d, tf), lambda i, f: (wi_gate, 0, f)),
        pl.BlockSpec((None, d, tf), lambda i, f: (wi_up, 0, f)),
        pl.BlockSpec((None, tf, d), lambda i, f: (wi_down, f, 0)),
        pl.BlockSpec((1, d), lambda i, f: (0, 0)),
    ]
    out_specs = [pl.BlockSpec((tm, d), lambda i, f: (i, 0))]
    out_shape = [jax.ShapeDtypeStruct((t, d), F32)]
    operands = [x, g_pre.reshape(1, d), w_gate, w_up, w_down, g_post.reshape(1, d)]
    n_cast = 0
    if cast_next is not None:
        *srcs, ci = cast_next
        n_cast = len(srcs)
        dr = d // nt
        in_specs += [pl.BlockSpec((None, dr, tf), lambda i, f: (ci, i, f)),
                     pl.BlockSpec((None, dr, tf), lambda i, f: (ci, i, f)),
                     pl.BlockSpec((None, tf, dr), lambda i, f: (ci, f, i))]
        out_specs += [pl.BlockSpec((dr, tf), lambda i, f: (i, f)), pl.BlockSpec((dr, tf), lambda i, f: (i, f)),
                      pl.BlockSpec((tf, dr), lambda i, f: (f, i))]
        out_shape += [jax.ShapeDtypeStruct(s.shape[-2:], BF16) for s in srcs]
        operands += srcs
        vmem += 2 * 3 * dr * tf * (4 + 2)
    if relayout_w_in is not None:
        w_in, l = relayout_w_in
        assert nt * nf >= RELAYOUT_BLOCKS
        tb = RELAYOUT_BLOCK
        blk = lambda i, f: jnp.minimum(i * nf + f, RELAYOUT_BLOCKS - 1)
        in_specs += [pl.BlockSpec((pl.Element(1), pl.Element(tb), pl.Element(d)),
                                  lambda i, f: (l, _relayout_src_start(blk(i, f)), 0)),
                     pl.BlockSpec((d, tb), lambda i, f: (0, 0))]
        out_specs += [pl.BlockSpec((d, tb), lambda i, f: (0, blk(i, f)))]
        out_shape += [jax.ShapeDtypeStruct((d, PROJ_COLS), BF16)]
        operands += [jnp.swapaxes(w_in, 1, 2), _w_in_tail(w_in, l)]
        vmem += 2 * 2 * tb * d * 4 + 2 * tb * d * 2 + tb * d * 4
    semantics = ("parallel", "arbitrary") if relayout_w_in is None else ("arbitrary", "arbitrary")
    return pl.pallas_call(
        functools.partial(_ffn_body, nf=nf, n_cast=n_cast, relayout=relayout_w_in is not None),
        grid=(nt, nf),
        in_specs=in_specs,
        out_specs=out_specs,
        out_shape=out_shape,
        scratch_shapes=[pltpu.VMEM((tm, d), BF16)],
        compiler_params=_cparams(semantics, vmem),
        name="ffn",
    )(*operands)


def _proj_body(x_ref, g_ref, w_ref, o_ref, xn_ref):
    @pl.when(pl.program_id(1) == 0)
    def _():
        xn_ref[...] = (_rms(x_ref[...]) * g_ref[...]).astype(BF16)

    o_ref[...] = _dot(xn_ref[...], w_ref[...])


def _norm_proj(x, g, w, *, tm=1024, tn=3 * PROJ_TILE):
    t, d = x.shape
    n = w.shape[1]
    tm = min(tm, t)
    vmem = tm * d * 4 + tm * d * 2 + 2 * d * tn * 2 + 3 * tm * tn * 4 + tm * d * 4
    return pl.pallas_call(
        _proj_body,
        grid=(t // tm, n // tn),
        in_specs=[
            pl.BlockSpec((tm, d), lambda i, j: (i, 0), pipeline_mode=pl.Buffered(1)),
            pl.BlockSpec((1, d), lambda i, j: (0, 0)),
            pl.BlockSpec((d, tn), lambda i, j: (0, j)),
        ],
        out_specs=pl.BlockSpec((tm, tn), lambda i, j: (i, j)),
        out_shape=jax.ShapeDtypeStruct((t, n), F32),
        scratch_shapes=[pltpu.VMEM((tm, d), BF16)],
        compiler_params=_cparams(("parallel", "arbitrary"), vmem),
        name="norm_proj",
    )(x, g.reshape(1, d), w)


def _rope_body(pos_ref, inv_ref, o_ref):
    ang = pos_ref[...] * inv_ref[...]
    lane = lax.broadcasted_iota(jnp.int32, ang.shape, 1)
    o_ref[...] = jnp.where(lane < 2 * (MLA_ROPE // 2), jnp.cos(ang), jnp.sin(ang))


def _rope_table(positions):
    b, s = positions.shape
    t = b * s
    inv_freq = jnp.power(ROPE_THETA, -jnp.arange(0, MLA_ROPE, 2, dtype=F32) / MLA_ROPE)
    inv = jnp.tile(inv_freq, LANES // (MLA_ROPE // 2)).reshape(1, LANES)
    pos = jnp.broadcast_to(positions.astype(F32).reshape(t, 1), (t, LANES))
    tm = min(1024, t)
    return pl.pallas_call(
        _rope_body,
        grid=(t // tm,),
        in_specs=[pl.BlockSpec((tm, LANES), lambda i: (i, 0)), pl.BlockSpec((1, LANES), lambda i: (0, 0))],
        out_specs=pl.BlockSpec((tm, LANES), lambda i: (i, 0)),
        out_shape=jax.ShapeDtypeStruct((t, LANES), F32),
        compiler_params=_cparams(("parallel",), 0),
        name="rope_table",
    )(pos, inv)


DN_HALO = SUBLANES


def _dn_body(alog_ref, dtb_ref, q_ref, k_ref, v_ref, z_ref, ab_ref, wq_ref, wk_ref, wv_ref, ng_ref, o_ref,
             state_ref, halo_ref):
    sblk = q_ref.shape[1]
    hd = DN_HEAD_DIM

    @pl.when(pl.program_id(1) == 0)
    def _():
        state_ref[...] = jnp.zeros_like(state_ref)
        halo_ref[...] = jnp.zeros_like(halo_ref)

    lane = lax.broadcasted_iota(jnp.int32, (1, LANES), 1)
    row = lax.broadcasted_iota(jnp.int32, (CHUNK, CHUNK), 0)
    col = lax.broadcasted_iota(jnp.int32, (CHUNK, CHUNK), 1)
    tril = (row >= col).astype(F32)
    row2 = lax.broadcasted_iota(jnp.int32, (2 * CHUNK, 2 * CHUNK), 0)
    col2i = lax.broadcasted_iota(jnp.int32, (2 * CHUNK, 2 * CHUNK), 1)
    same_head = (row2 // CHUNK) == (col2i // CHUNK)
    incl2 = same_head & (row2 >= col2i)
    strict2 = same_head & (row2 > col2i)
    a_scale = jnp.where(lane < DN_HEADS, -jnp.exp(alog_ref[...]), 0.0)
    dt_bias = dtb_ref[...]
    ng = ng_ref[...]
    seq_refs = (q_ref, k_ref, v_ref)
    w_refs = (wq_ref, wk_ref, wv_ref)

    def conv_silu(which, sl, c, c0):
        ref = seq_refs[which]
        cur = ref[0, pl.ds(c0, CHUNK), sl]
        p0 = pl.multiple_of(jnp.maximum(c0 - DN_HALO, 0), DN_HALO)
        prev = jnp.where(c > 0, ref[0, pl.ds(p0, DN_HALO), sl], halo_ref[which, :, sl])
        xx = jnp.concatenate([prev, cur], axis=0)
        w = w_refs[which][:, sl]
        first = DN_HALO - (DN_CONV - 1)
        y = w[0:1] * xx[first:first + CHUNK]
        for j in range(1, DN_CONV):
            y = y + w[j:j + 1] * xx[first + j:first + j + CHUNK]
        return _silu(y)

    heads = range(DN_HEADS)
    pairs = range(DN_HEADS // 2)
    unroll = 4
    items = [(ci, pr) for ci in range(unroll) for pr in pairs]
    sls = [slice(h * hd, (h + 1) * hd) for h in heads]
    top, bot = slice(0, CHUNK), slice(CHUNK, 2 * CHUNK)
    half = lambda h: top if h % 2 == 0 else bot

    def col2(a, off, pr):
        return jnp.concatenate([a[:, off + 2 * pr:off + 2 * pr + 1], a[:, off + 2 * pr + 1:off + 2 * pr + 2]], axis=0)

    def gate_terms(c0):
        ab = ab_ref[0, pl.ds(c0, CHUNK), :]
        xa = ab + dt_bias
        g_all = a_scale * (jnp.maximum(xa, 0.0) + jnp.log1p(jnp.exp(-jnp.abs(xa))))
        beta_all = jax.nn.sigmoid(ab)
        gc_all = jnp.dot(tril, g_all, precision=lax.Precision.HIGHEST, preferred_element_type=F32)
        g_last = gc_all[CHUNK - 1:CHUNK, :]
        gc_t = jnp.concatenate([gc_all, gc_all], axis=0).T
        return dict(beta=beta_all, gc=gc_all, gc_t=gc_t, eg=jnp.exp(gc_all), ed=jnp.exp(g_last - gc_all),
                    egl=jnp.exp(g_last))

    def chunk_group_step(it, carry):
        cidx = [it * unroll + ci for ci in range(unroll)]
        c0s = [pl.multiple_of(c * CHUNK, CHUNK) for c in cidx]
        gt = [gate_terms(c0) for c0 in c0s]

        def stacked(fn, ci, pr):
            return jnp.concatenate([fn(ci, 2 * pr), fn(ci, 2 * pr + 1)], axis=0)

        z2 = {(ci, pr): stacked(lambda ci, h: z_ref[0, pl.ds(c0s[ci], CHUNK), sls[h]], ci, pr) for ci, pr in items}
        q2 = {(ci, pr): stacked(lambda ci, h: conv_silu(0, sls[h], cidx[ci], c0s[ci]), ci, pr) for ci, pr in items}
        k2 = {(ci, pr): stacked(lambda ci, h: conv_silu(1, sls[h], cidx[ci], c0s[ci]), ci, pr) for ci, pr in items}
        v2 = {(ci, pr): stacked(lambda ci, h: conv_silu(2, sls[h], cidx[ci], c0s[ci]), ci, pr) for ci, pr in items}
        q2 = {i: x * lax.rsqrt(jnp.sum(x * x, axis=-1, keepdims=True) + 1e-6) * (hd ** -0.5) for i, x in q2.items()}
        k2 = {i: x * lax.rsqrt(jnp.sum(x * x, axis=-1, keepdims=True) + 1e-6) for i, x in k2.items()}
        beta2 = {(ci, pr): col2(gt[ci]["beta"], DN_HEADS, pr) for ci, pr in items}
        eg2 = {(ci, pr): col2(gt[ci]["eg"], 0, pr) for ci, pr in items}
        g_row2 = {(ci, pr): jnp.where(lane < CHUNK, gt[ci]["gc_t"][2 * pr:2 * pr + 1], gt[ci]["gc_t"][2 * pr + 1:2 * pr + 2])
                  for ci, pr in items}
        decay2 = {(ci, pr): jnp.exp(jnp.where(incl2, col2(gt[ci]["gc"], 0, pr) - g_row2[ci, pr], -jnp.inf))
                  for ci, pr in items}
        kb2 = {i: k2[i] * beta2[i] for i in items}
        km2 = {i: k2[i].astype(BF16) for i in items}
        kk2 = {i: _dot_nt(jnp.concatenate([kb2[i], q2[i]], axis=0).astype(BF16), km2[i]) for i in items}
        qk2 = {i: (kk2[i][2 * CHUNK:] * decay2[i]).astype(BF16) for i in items}
        p2 = {i: jnp.where(strict2, -(kk2[i][:2 * CHUNK] * decay2[i]), 0.0) for i in items}
        sol2 = {i: jnp.concatenate([kb2[i] * eg2[i], v2[i] * beta2[i]], axis=1) for i in items}
        sol2 = {i: sol2[i] + _dot(p2[i].astype(BF16), sol2[i].astype(BF16)) for i in items}
        for _ in range(5):
            pb = {i: p2[i].astype(BF16) for i in items}
            p2 = {i: _dot(pb[i], pb[i]) for i in items}
            sol2 = {i: sol2[i] + _dot(p2[i].astype(BF16), sol2[i].astype(BF16)) for i in items}
        wq2 = {(ci, h): jnp.concatenate([sol2[ci, h // 2][half(h), :hd], (q2[ci, h // 2] * eg2[ci, h // 2])[half(h)]],
                                        axis=0).astype(BF16) for ci in range(unroll) for h in heads}
        kd2 = {(ci, pr): (k2[ci, pr] * col2(gt[ci]["ed"], 0, pr)).astype(BF16) for ci, pr in items}
        states = [state_ref[h] for h in heads]
        for ci in range(unroll):
            ws = [_dot(wq2[ci, h], states[h].astype(BF16)) for h in heads]
            vn2 = [(sol2[ci, pr][:, hd:] - jnp.concatenate([ws[2 * pr][:CHUNK], ws[2 * pr + 1][:CHUNK]], axis=0)).astype(BF16)
                   for pr in pairs]
            states = [states[h] * gt[ci]["egl"][:, h:h + 1] + _dot_tn(kd2[ci, h // 2][half(h)], vn2[h // 2][half(h)])
                      for h in heads]
            o2 = [jnp.concatenate([ws[2 * pr][CHUNK:], ws[2 * pr + 1][CHUNK:]], axis=0) + _dot(qk2[ci, pr], vn2[pr])
                  for pr in pairs]
            out2 = [(_rms(o2[pr]) * ng * _silu(z2[ci, pr])).astype(o_ref.dtype) for pr in pairs]
            for h in heads:
                o_ref[0, pl.ds(c0s[ci], CHUNK), sls[h]] = out2[h // 2][half(h)]
        for h in heads:
            state_ref[h] = states[h]
        return carry

    lax.fori_loop(0, sblk // (CHUNK * unroll), chunk_group_step, 0)
    for which in range(3):
        halo_ref[which] = seq_refs[which][0, sblk - DN_HALO:sblk, :]


def _deltanet(proj3, conv_w, a_log, dt_bias, norm_g, *, sblk=512):
    b, s, _ = proj3.shape
    sblk = min(sblk, s)
    wd = DN_WIDTH
    pad = lambda vec: jnp.concatenate([vec, jnp.zeros((LANES - DN_HEADS,), F32)]).reshape(1, LANES)
    seq_spec = lambda blk: pl.BlockSpec((1, sblk, wd), lambda i, j: (i, j, blk))
    w_spec = lambda blk: pl.BlockSpec((DN_CONV, wd), lambda i, j: (0, blk))
    vec_spec = pl.BlockSpec((1, LANES), lambda i, j: (0, 0))
    live_chunk_values = 16 * 1024 * 1024
    vmem = 2 * 4 * sblk * wd * 4 + 2 * sblk * wd * 2 + 2 * sblk * LANES * 4 + live_chunk_values
    return pl.pallas_call(
        _dn_body,
        grid=(b, s // sblk),
        in_specs=[vec_spec, vec_spec,
                  seq_spec(OFF_QKV // wd), seq_spec(OFF_QKV // wd + 1), seq_spec(OFF_QKV // wd + 2), seq_spec(OFF_Z // wd),
                  pl.BlockSpec((1, sblk, LANES), lambda i, j: (i, j, OFF_AB // LANES)),
                  w_spec(0), w_spec(1), w_spec(2), vec_spec],
        out_specs=pl.BlockSpec((1, sblk, wd), lambda i, j: (i, j, 0)),
        out_shape=jax.ShapeDtypeStruct((b, s, wd), BF16),
        scratch_shapes=[pltpu.VMEM((DN_HEADS, DN_HEAD_DIM, DN_HEAD_DIM), F32), pltpu.VMEM((3, DN_HALO, wd), F32)],
        compiler_params=_cparams(("parallel", "arbitrary"), vmem),
        name="deltanet",
    )(pad(a_log), pad(dt_bias), proj3, proj3, proj3, proj3, proj3, conv_w, conv_w, conv_w, norm_g.reshape(1, -1))


def _gelu(x):
    return 0.5 * x * (1.0 + lax.erf(x * (0.5 ** 0.5)))


def _sg_body(u_ref, v_ref, lng_ref, lnb_ref, w_ref, bt_ref, o_ref):
    u = _gelu(u_ref[0])
    v = _gelu(v_ref[0])
    mu = jnp.mean(v, axis=-1, keepdims=True)
    vc = v - mu
    var = jnp.mean(vc * vc, axis=-1, keepdims=True)
    vn = (vc * lax.rsqrt(var + 1e-5) * lng_ref[...] + lnb_ref[...]).astype(BF16)
    row = lax.broadcasted_iota(jnp.int32, (SG_BLOCK, SG_BLOCK), 0)
    col = lax.broadcasted_iota(jnp.int32, (SG_BLOCK, SG_BLOCK), 1)
    mask = (col // CHUNK) <= (row // CHUNK)
    for g in range(SG_GROUPS):
        sl = slice(g * SG_GROUP_DIM, (g + 1) * SG_GROUP_DIM)
        wg = jnp.where(mask, w_ref[g], 0.0).astype(BF16)
        for blk in range(u.shape[0] // SG_BLOCK):
            rows = slice(blk * SG_BLOCK, (blk + 1) * SG_BLOCK)
            mixed = _dot(wg, vn[rows, sl]) + bt_ref[:, g:g + 1]
            o_ref[0, rows, sl] = (u[rows, sl] * mixed).astype(o_ref.dtype)


def _gmlp(proj3, ln_g, ln_b, sg_w, sg_b, *, blocks_per_step=2):
    b, s, _ = proj3.shape
    ub = OFF_SG // SG_WIDTH
    rows = blocks_per_step * SG_BLOCK
    vmem = 2 * 2 * rows * SG_WIDTH * 4 + 2 * SG_GROUPS * SG_BLOCK * SG_BLOCK * 4 + 8 * rows * SG_WIDTH * 4
    return pl.pallas_call(
        _sg_body,
        grid=(b, s // rows),
        in_specs=[
            pl.BlockSpec((1, rows, SG_WIDTH), lambda i, n: (i, n, ub)),
            pl.BlockSpec((1, rows, SG_WIDTH), lambda i, n: (i, n, ub + 1)),
            pl.BlockSpec((1, SG_WIDTH), lambda i, n: (0, 0)),
            pl.BlockSpec((1, SG_WIDTH), lambda i, n: (0, 0)),
            pl.BlockSpec((SG_GROUPS, SG_BLOCK, SG_BLOCK), lambda i, n: (0, 0, 0)),
            pl.BlockSpec((SG_BLOCK, SG_GROUPS), lambda i, n: (0, 0)),
        ],
        out_specs=pl.BlockSpec((1, rows, SG_WIDTH), lambda i, n: (i, n, 0)),
        out_shape=jax.ShapeDtypeStruct((b, s, SG_WIDTH), BF16),
        compiler_params=_cparams(("parallel", "parallel"), vmem),
        name="gmlp",
    )(proj3, proj3, ln_g.reshape(1, SG_WIDTH), ln_b.reshape(1, SG_WIDTH), sg_w, sg_b.T)


def _rotate(block, cs):
    t = block * cs
    return t + pltpu.roll(t, MLA_ROPE, 1)


def _mla_proj_body(cq_ref, ckv_ref, kr_ref, cs_ref, gq_ref, gkv_ref, wq_ref, wk_ref, wvt_ref, q_out, k_out, vt_out):
    cs = cs_ref[0]
    scale = (MLA_NOPE + MLA_ROPE) ** -0.5
    cqn = (_rms(cq_ref[0]) * gq_ref[...]).astype(BF16)
    q = _dot(cqn, wq_ref[...])
    lane = lax.broadcasted_iota(jnp.int32, cs.shape, 1)
    k_rope = jnp.where(lane < MLA_ROPE, _rotate(kr_ref[0], cs), 0.0).astype(k_out.dtype)
    ckvn = (_rms(ckv_ref[0]) * gkv_ref[...]).astype(BF16)
    k = _dot(ckvn, wk_ref[...])
    for h in range(MLA_HEADS):
        base = h * MLA_QW
        q_out[0, :, base:base + MLA_NOPE] = (q[:, base:base + MLA_NOPE] * scale).astype(q_out.dtype)
        q_out[0, :, base + MLA_NOPE:base + MLA_QW] = (_rotate(q[:, base + MLA_NOPE:base + MLA_QW], cs) * scale).astype(q_out.dtype)
        k_out[0, :, base:base + MLA_NOPE] = k[:, h * MLA_NOPE:(h + 1) * MLA_NOPE].astype(k_out.dtype)
        k_out[0, :, base + MLA_NOPE:base + MLA_QW] = k_rope
    vt_out[0] = _dot_nt(wvt_ref[...], ckvn).astype(vt_out.dtype)


def _mla_proj(proj3, cs3, cq_g, ckv_g, w_uq, w_uk, w_uvt, *, tm=512):
    b, s, _ = proj3.shape
    tm = min(tm, s)
    nq = w_uq.shape[1]
    nk = MLA_HEADS * MLA_QW
    nv = w_uvt.shape[0]
    vmem = (2 * tm * (MLA_Q_RANK + MLA_KV_RANK + 2 * LANES) * 4 + 2 * MLA_Q_RANK * (nq + w_uk.shape[1] + nv) * 2
            + 2 * tm * (nq + nk + nv) * 2 + 3 * tm * (nq + nk + nv) * 4)
    const = lambda shape: pl.BlockSpec(shape, lambda bi, i: (0, 0))
    return pl.pallas_call(
        _mla_proj_body,
        grid=(b, s // tm),
        in_specs=[
            pl.BlockSpec((1, tm, MLA_Q_RANK), lambda bi, i: (bi, i, OFF_CQ // MLA_Q_RANK)),
            pl.BlockSpec((1, tm, MLA_KV_RANK), lambda bi, i: (bi, i, OFF_CKV // MLA_KV_RANK)),
            pl.BlockSpec((1, tm, LANES), lambda bi, i: (bi, i, OFF_KR // LANES)),
            pl.BlockSpec((1, tm, LANES), lambda bi, i: (bi, i, 0)),
            const((1, MLA_Q_RANK)), const((1, MLA_KV_RANK)),
            const(w_uq.shape), const(w_uk.shape), const(w_uvt.shape),
        ],
        out_specs=[
            pl.BlockSpec((1, tm, nq), lambda bi, i: (bi, i, 0)),
            pl.BlockSpec((1, tm, nk), lambda bi, i: (bi, i, 0)),
            pl.BlockSpec((1, nv, tm), lambda bi, i: (bi, 0, i)),
        ],
        out_shape=[jax.ShapeDtypeStruct((b, s, nq), BF16), jax.ShapeDtypeStruct((b, s, nk), BF16),
                   jax.ShapeDtypeStruct((b, nv, s), BF16)],
        compiler_params=_cparams(("parallel", "parallel"), vmem),
        name="mla_proj",
    )(proj3, proj3, proj3, cs3, cq_g.reshape(1, -1), ckv_g.reshape(1, -1), w_uq, w_uk, w_uvt)


ATTN_HEADS_PER_STEP = 8


def _attn_body(q_ref, k_ref, vt_ref, o_ref, *, tq, nh):
    i = pl.program_id(2)
    diag_mask = (lax.broadcasted_iota(jnp.int32, (tq, tq), 0) // CHUNK
                 <= lax.broadcasted_iota(jnp.int32, (tq, tq), 1) // CHUNK)

    def kv_step(j, carry, masked):
        k0 = pl.multiple_of(j * tq, tq)
        heads = range(nh)
        st = [_dot_nt(k_ref[0, pl.ds(k0, tq), h * MLA_QW:(h + 1) * MLA_QW], q_ref[0, :, h * MLA_QW:(h + 1) * MLA_QW])
              for h in heads]
        if masked:
            st = [jnp.where(diag_mask, x, -1e30) for x in st]
        m_new = [jnp.maximum(carry[h][0], jnp.max(st[h], axis=0, keepdims=True)) for h in heads]
        alpha = [jnp.exp(carry[h][0] - m_new[h]) for h in heads]
        pt = [jnp.exp(st[h] - m_new[h]) for h in heads]
        l = [alpha[h] * carry[h][1] + jnp.sum(pt[h], axis=0, keepdims=True) for h in heads]
        pv = [_dot(vt_ref[0, h * MLA_V:(h + 1) * MLA_V, pl.ds(k0, tq)], pt[h].astype(BF16)) for h in heads]
        return tuple((m_new[h], l[h], alpha[h] * carry[h][2] + pv[h]) for h in heads)

    init = tuple((jnp.full((1, tq), -jnp.inf, F32), jnp.zeros((1, tq), F32), jnp.zeros((MLA_V, tq), F32))
                 for _ in range(nh))
    carry = lax.fori_loop(0, i, functools.partial(kv_step, masked=False), init)
    carry = kv_step(i, carry, masked=True)
    for h in range(nh):
        _, l, acc = carry[h]
        o_ref[0, :, h * MLA_V:(h + 1) * MLA_V] = (acc / l).T.astype(o_ref.dtype)


def _attention(q3, k3, vt3, *, tq=256, nh=ATTN_HEADS_PER_STEP):
    b, s, _ = q3.shape
    tq = min(tq, s)
    vmem = (2 * tq * nh * MLA_QW * 2 + 2 * s * nh * MLA_QW * 2 + 2 * s * nh * MLA_V * 2 + 2 * tq * nh * MLA_V * 2
            + nh * 8 * tq * tq * 4)
    return pl.pallas_call(
        functools.partial(_attn_body, tq=tq, nh=nh),
        grid=(b, MLA_HEADS // nh, s // tq),
        in_specs=[
            pl.BlockSpec((1, tq, nh * MLA_QW), lambda bi, hg, i: (bi, i, hg)),
            pl.BlockSpec((1, s, nh * MLA_QW), lambda bi, hg, i: (bi, 0, hg)),
            pl.BlockSpec((1, nh * MLA_V, s), lambda bi, hg, i: (bi, hg, 0)),
        ],
        out_specs=pl.BlockSpec((1, tq, nh * MLA_V), lambda bi, hg, i: (bi, i, hg)),
        out_shape=jax.ShapeDtypeStruct((b, s, MLA_HEADS * MLA_V), BF16),
        compiler_params=_cparams(("parallel", "parallel", "arbitrary"), vmem),
        name="mla_attention",
    )(q3, k3, vt3)


def _merge_body(x_ref, oa_ref, ob_ref, oc_ref, ga_ref, gb_ref, gc_ref, wa_ref, wb_ref, wc_ref, wo_ref, g_ref,
                o_ref, xs_ref, *, nc):
    c = pl.program_id(1)
    xs_ref[c] = x_ref[...]

    @pl.when(c == 0)
    def _():
        o_ref[...] = jnp.zeros_like(o_ref)

    merged = jax.nn.sigmoid(ga_ref[...]) * _dot(oa_ref[...], wa_ref[...])
    merged = merged + jax.nn.sigmoid(gb_ref[...]) * _dot(ob_ref[...], wb_ref[...])
    merged = merged + jax.nn.sigmoid(gc_ref[...]) * _dot(oc_ref[...], wc_ref[...])
    o_ref[...] += _dot(merged.astype(BF16), wo_ref[...])

    @pl.when(c == nc - 1)
    def _():
        x = jnp.concatenate([xs_ref[k] for k in range(nc)], axis=1)
        o_ref[...] = x + _rms(o_ref[...]) * g_ref[...]


def _merge(x, oa, ob, oc, proj, w_branch, w_out, l, g, *, tm=512, tc=512):
    t, d = x.shape
    tm = min(tm, t)
    nc = d // tc
    bw = oa.shape[1]
    vmem = (2 * tm * d * 4 + 2 * tm * d * 4 + 2 * 3 * tm * bw * 2 + 2 * 3 * tm * tc * 4 + 2 * 3 * bw * tc * 2
            + 2 * tc * d * 2 + 6 * tm * tc * 4 + tm * d * 4)
    o_spec = pl.BlockSpec((tm, bw), lambda i, c: (i, 0))
    gate_blk = OFF_GATE // tc
    gate_spec = lambda n: pl.BlockSpec((tm, tc), lambda i, c: (i, gate_blk + n * nc + c))
    wb_spec = lambda n: pl.BlockSpec((None, bw, tc), lambda i, c: (N_BRANCH * l + n, 0, c))
    return pl.pallas_call(
        functools.partial(_merge_body, nc=nc),
        grid=(t // tm, nc),
        in_specs=[pl.BlockSpec((tm, tc), lambda i, c: (i, c)), o_spec, o_spec, o_spec,
                  gate_spec(0), gate_spec(1), gate_spec(2), wb_spec(0), wb_spec(1), wb_spec(2),
                  pl.BlockSpec((None, tc, d), lambda i, c: (l, c, 0)),
                  pl.BlockSpec((1, d), lambda i, c: (0, 0))],
        out_specs=pl.BlockSpec((tm, d), lambda i, c: (i, 0)),
        out_shape=jax.ShapeDtypeStruct((t, d), F32),
        scratch_shapes=[pltpu.VMEM((nc, tm, tc), F32)],
        compiler_params=_cparams(("parallel", "arbitrary"), vmem),
        name="merge",
    )(x, oa, ob, oc, proj, proj, proj, w_branch, w_branch, w_branch, w_out, g.reshape(1, d))


def kernel(x, positions, norm_g, ffn_w_gate, ffn_w_up, ffn_w_down, w_in, dn_conv_w, dn_a_log, dn_dt_bias, dn_norm_g, sg_ln_g, sg_ln_b, sg_w, sg_b, mla_cq_norm_g, mla_ckv_norm_g, mla_w_uq, mla_w_ukv, w_branch, w_out):
    b, s, d = x.shape
    t = b * s
    xf = x.reshape(t, d)
    cs = _rope_table(positions)
    wb_all, wo_all = _cast_bf16(w_branch), _cast_bf16(w_out)
    ffn_f32 = tuple(w.reshape((2 * DEPTH,) + w.shape[2:]) for w in (ffn_w_gate, ffn_w_up, ffn_w_down))
    gate_up = _cast_pair_bf16(ffn_f32[0], ffn_f32[1], 0)
    ffn_bf16, wi = [gate_up, gate_up, _cast_bf16(ffn_f32[2], only=0)], (0, 1, 0)
    for l in range(DEPTH):
        ng = norm_g[l]
        xf, *nxt, w_in_bf16 = _ffn(xf, ng[0], *ffn_bf16, wi, ng[1], cast_next=ffn_f32 + (2 * l + 1,),
                                   relayout_w_in=(w_in, l))
        ffn_bf16, wi = [w[None] for w in nxt], (0, 0, 0)
        proj = _norm_proj(xf, ng[2], w_in_bf16)
        proj3 = proj.reshape(b, s, PROJ_COLS)
        o_a = _deltanet(proj3, dn_conv_w[l], dn_a_log[l], dn_dt_bias[l], dn_norm_g[l])
        o_b = _gmlp(proj3, sg_ln_g[l], sg_ln_b[l], sg_w[l], sg_b[l])
        w_uk, w_uvt = _split_w_ukv(mla_w_ukv[l])
        q, k, vt = _mla_proj(proj3, cs.reshape(b, s, LANES), mla_cq_norm_g[l], mla_ckv_norm_g[l],
                             _relayout_w_uq(mla_w_uq[l]), w_uk, w_uvt)
        o_c = _attention(q, k, vt)
        xf = _merge(xf, o_a.reshape(t, -1), o_b.reshape(t, -1), o_c.reshape(t, -1), proj, wb_all, wo_all, l, ng[3])
        xf, *nxt = _ffn(xf, ng[4], *ffn_bf16, wi, ng[5],
                        cast_next=ffn_f32 + (2 * l + 2,) if l + 1 < DEPTH else None)
        ffn_bf16 = [w[None] for w in nxt]
    return xf.reshape(b, s, d)
```

```python
import functools

import jax
import jax.numpy as jnp
from jax import lax
from jax.experimental import pallas as pl
from jax.experimental.pallas import tpu as pltpu

F32 = jnp.float32
BF16 = jnp.bfloat16

D_MODEL = 2048
DEPTH = 2
CHUNK = 64
NORM_EPS = 1e-6
DN_HEADS = 8
DN_HEAD_DIM = 128
DN_WIDTH = DN_HEADS * DN_HEAD_DIM
DN_CONV = 4
SG_GROUPS = 8
SG_GROUP_DIM = 128
SG_WIDTH = SG_GROUPS * SG_GROUP_DIM
SG_BLOCK = 128
MLA_HEADS = 8
MLA_Q_RANK = 512
MLA_KV_RANK = 512
MLA_NOPE = 128
MLA_ROPE = 64
MLA_V = 128
ROPE_THETA = 10000.0
N_BRANCH = 3
BRANCH_WIDTH = 1024
D_FF = 5632

LANES = 128
SUBLANES = 8
V7X_VMEM_BYTES = 64 * 1024 * 1024
VMEM_CAP = V7X_VMEM_BYTES - 6 * 1024 * 1024

OFF_QKV = 0
OFF_Z = OFF_QKV + 3 * DN_WIDTH
OFF_SG = OFF_Z + DN_WIDTH
OFF_GATE = OFF_SG + 2 * SG_WIDTH
OFF_CQ = OFF_GATE + N_BRANCH * D_MODEL
OFF_CKV = OFF_CQ + MLA_Q_RANK
OFF_AB = OFF_CKV + MLA_KV_RANK
OFF_KR = OFF_AB + LANES
PROJ_TILE = 512
PROJ_COLS = -(-(OFF_KR + LANES) // PROJ_TILE) * PROJ_TILE
PROJ_TAIL = PROJ_COLS - OFF_AB
MLA_QW = 2 * LANES

SRC_AB = 4 * DN_WIDTH
SRC_SG = SRC_AB + 2 * DN_HEADS
SRC_CQ = SRC_SG + 2 * SG_WIDTH
SRC_CKV = SRC_CQ + MLA_Q_RANK
SRC_KR = SRC_CKV + MLA_KV_RANK
SRC_GATE = SRC_KR + MLA_ROPE
IN_COLS = SRC_GATE + N_BRANCH * D_MODEL


def _cparams(semantics, vmem_bytes):
    return pltpu.CompilerParams(dimension_semantics=semantics,
                                vmem_limit_bytes=int(min(max(vmem_bytes, 16 * 1024 * 1024), VMEM_CAP)))


def _dot(a, b):
    return jnp.dot(a, b, preferred_element_type=F32)


def _dot_nt(a, b):
    return lax.dot_general(a, b, (((1,), (1,)), ((), ())), preferred_element_type=F32)


def _dot_tn(a, b):
    return lax.dot_general(a, b, (((0,), (0,)), ((), ())), preferred_element_type=F32)


def _rms(x, eps=NORM_EPS):
    return x * lax.rsqrt(jnp.mean(x * x, axis=-1, keepdims=True) + eps)


def _silu(x):
    return x * jax.nn.sigmoid(x)


def _cast_body(w_ref, o_ref):
    o_ref[...] = w_ref[...].astype(o_ref.dtype)


def _cast_bf16(w, only=None, *, block_bytes=4 * 1024 * 1024):
    r, c = w.shape[-2:]
    w = w.reshape(-1, r, c)
    n, first = (w.shape[0], 0) if only is None else (1, only)
    rb = r
    while rb * c * 4 > block_bytes and rb % (4 * SUBLANES) == 0:
        rb //= 2
    return pl.pallas_call(
        _cast_body,
        grid=(n, r // rb),
        in_specs=[pl.BlockSpec((1, rb, c), lambda m, i: (first + m, i, 0))],
        out_specs=pl.BlockSpec((1, rb, c), lambda m, i: (m, i, 0)),
        out_shape=jax.ShapeDtypeStruct((n, r, c), BF16),
        compiler_params=_cparams(("parallel", "parallel"), 3 * 2 * rb * c * 4),
        name="cast_bf16",
    )(w)


_WIDE_SEGMENTS = ((OFF_QKV, 0, SRC_AB), (OFF_SG, SRC_SG, 2 * SG_WIDTH), (OFF_GATE, SRC_GATE, N_BRANCH * D_MODEL),
                  (OFF_CQ, SRC_CQ, MLA_Q_RANK), (OFF_CKV, SRC_CKV, MLA_KV_RANK))


def _cast_pair_body(a_ref, b_ref, o_ref):
    o_ref[0] = a_ref[0].astype(o_ref.dtype)
    o_ref[1] = b_ref[0].astype(o_ref.dtype)


def _cast_pair_bf16(a, b, k, *, block_bytes=2 * 1024 * 1024):
    n, r, c = a.shape
    rb = r
    while rb * c * 4 > block_bytes and rb % (4 * SUBLANES) == 0:
        rb //= 2
    src = pl.BlockSpec((1, rb, c), lambda i: (k, i, 0))
    return pl.pallas_call(
        _cast_pair_body,
        grid=(r // rb,),
        in_specs=[src, src],
        out_specs=pl.BlockSpec((2, rb, c), lambda i: (0, i, 0)),
        out_shape=jax.ShapeDtypeStruct((2, r, c), BF16),
        compiler_params=_cparams(("parallel",), 3 * 2 * 2 * rb * c * 4),
        name="cast_pair_bf16",
    )(a, b)


RELAYOUT_BLOCK = PROJ_TAIL
RELAYOUT_WIDE_BLOCKS = OFF_AB // RELAYOUT_BLOCK
RELAYOUT_BLOCKS = PROJ_COLS // RELAYOUT_BLOCK


def _relayout_step(step, wt_ref, tail_ref, o_ref):
    @pl.when(step < RELAYOUT_WIDE_BLOCKS)
    def _():
        o_ref[...] = wt_ref[0].T.astype(BF16)

    @pl.when(step == RELAYOUT_WIDE_BLOCKS)
    def _():
        o_ref[...] = tail_ref[...].astype(BF16)


def _relayout_src_start(blk):
    dst = blk * RELAYOUT_BLOCK
    start = 0
    for d0, s0, width in _WIDE_SEGMENTS:
        start = jnp.where((dst >= d0) & (dst < d0 + width), s0 + dst - d0, start)
    return pl.multiple_of(start, 2 * SUBLANES)


def _swap_halves(w):
    half = w.shape[-1] // 2
    return jnp.concatenate([-w[..., half:], w[..., :half]], axis=-1)


def _w_in_tail(w_in, l):
    d = w_in.shape[1]
    kr = w_in[l, :, SRC_KR:SRC_GATE]
    return jnp.concatenate([w_in[l, :, SRC_AB:SRC_SG], jnp.zeros((d, LANES - 2 * DN_HEADS), F32), kr, _swap_halves(kr),
                            jnp.zeros((d, PROJ_TAIL - 2 * LANES), F32)], axis=1)


def _relayout_w_uq(w):
    r = w.shape[0]
    w = w.reshape(r, MLA_HEADS, MLA_NOPE + MLA_ROPE)
    rope = w[..., MLA_NOPE:]
    return jnp.concatenate([w[..., :MLA_NOPE], rope, _swap_halves(rope)], axis=-1).reshape(r, MLA_HEADS * MLA_QW).astype(BF16)


def _split_w_ukv(w):
    r = w.shape[0]
    w = w.reshape(r, MLA_HEADS, MLA_NOPE + MLA_V)
    w_uk = w[..., :MLA_NOPE].reshape(r, MLA_HEADS * MLA_NOPE).astype(BF16)
    w_uvt = w[..., MLA_NOPE:].reshape(r, MLA_HEADS * MLA_V).T.astype(BF16)
    return w_uk, w_uvt


def _ffn_body(x_ref, gpre_ref, wg_ref, wu_ref, wd_ref, gpost_ref, *rest, nf, n_cast, relayout):
    n_in = n_cast + (3 if relayout else 0)
    cast_src, o_ref, cast_dst, xn_ref = rest[:n_cast], rest[n_in], rest[n_in + 1:n_in + 1 + n_cast], rest[-1]
    f = pl.program_id(1)

    @pl.when(f == 0)
    def _():
        xn_ref[...] = (_rms(x_ref[...]) * gpre_ref[...]).astype(BF16)
        o_ref[...] = jnp.zeros_like(o_ref)

    xn = xn_ref[...]
    h = _dot(xn, wg_ref[...])
    u = _dot(xn, wu_ref[...])
    a = (_silu(h) * u).astype(BF16)
    o_ref[...] += _dot(a, wd_ref[...])
    for src, dst in zip(cast_src, cast_dst):
        dst[...] = src[...].astype(dst.dtype)
    if relayout:
        _relayout_step(pl.program_id(0) * nf + f, rest[n_cast], rest[n_cast + 1], rest[n_in + 1 + n_cast])

    @pl.when(f == nf - 1)
    def _():
        y = x_ref[...] + 0.5 * (_rms(o_ref[...]) * gpost_ref[...])
        o_ref[...] = y
        if relayout:
            rest[n_in + 2 + n_cast][...] = (_rms(y) * rest[n_cast + 2][...]).astype(BF16)


def _ffn(x, g_pre, w_gate, w_up, w_down, wi, g_post, cast_next=None, relayout_w_in=None, *, tm=512, tf=512):
    t, d = x.shape
    tm = min(tm, t)
    dff = w_gate.shape[-1]
    nf = dff // tf
    nt = t // tm
    wi_gate, wi_up, wi_down = wi
    vmem = (2 * tm * d * 4 + 2 * tm * d * 4 + tm * d * 2 + 2 * 3 * d * tf * 2 + 5 * tm * tf * 4 + tm * d * 4)
    in_specs = [
        pl.BlockSpec((tm, d), lambda i, f: (i, 0)),
        pl.BlockSpec((1, d), lambda i, f: (0, 0)),
        pl.BlockSpec((None, d, tf), lambda i, f: (wi_gate, 0, f)),
        pl.BlockSpec((None, d, tf), lambda i, f: (wi_up, 0, f)),
        pl.BlockSpec((None, tf, d), lambda i, f: (wi_down, f, 0)),
        pl.BlockSpec((1, d), lambda i, f: (0, 0)),
    ]
    out_specs = [pl.BlockSpec((tm, d), lambda i, f: (i, 0))]
    out_shape = [jax.ShapeDtypeStruct((t, d), F32)]
    operands = [x, g_pre.reshape(1, d), w_gate, w_up, w_down, g_post.reshape(1, d)]
    n_cast = 0
    if cast_next is not None:
        *srcs, ci = cast_next
        n_cast = len(srcs)
        dr = d // nt
        in_specs += [pl.BlockSpec((None, dr, tf), lambda i, f: (ci, i, f)),
                     pl.BlockSpec((None, dr, tf), lambda i, f: (ci, i, f)),
                     pl.BlockSpec((None, tf, dr), lambda i, f: (ci, f, i))]
        out_specs += [pl.BlockSpec((dr, tf), lambda i, f: (i, f)), pl.BlockSpec((dr, tf), lambda i, f: (i, f)),
                      pl.BlockSpec((tf, dr), lambda i, f: (f, i))]
        out_shape += [jax.ShapeDtypeStruct(s.shape[-2:], BF16) for s in srcs]
        operands += srcs
        vmem += 2 * 3 * dr * tf * (4 + 2)
    if relayout_w_in is not None:
        w_in, l, g_proj = relayout_w_in
        assert nt * nf >= RELAYOUT_BLOCKS
        tb = RELAYOUT_BLOCK
        blk = lambda i, f: jnp.minimum(i * nf + f, RELAYOUT_BLOCKS - 1)
        in_specs += [pl.BlockSpec((pl.Element(1), pl.Element(tb), pl.Element(d)),
                                  lambda i, f: (l, _relayout_src_start(blk(i, f)), 0)),
                     pl.BlockSpec((d, tb), lambda i, f: (0, 0)),
                     pl.BlockSpec((1, d), lambda i, f: (0, 0))]
        out_specs += [pl.BlockSpec((d, tb), lambda i, f: (0, blk(i, f))), pl.BlockSpec((tm, d), lambda i, f: (i, 0))]
        out_shape += [jax.ShapeDtypeStruct((d, PROJ_COLS), BF16), jax.ShapeDtypeStruct((t, d), BF16)]
        operands += [jnp.swapaxes(w_in, 1, 2), _w_in_tail(w_in, l), g_proj.reshape(1, d)]
        vmem += 2 * 2 * tb * d * 4 + 2 * tb * d * 2 + tb * d * 4 + 2 * tm * d * 2
    semantics = ("parallel", "arbitrary") if relayout_w_in is None else ("arbitrary", "arbitrary")
    return pl.pallas_call(
        functools.partial(_ffn_body, nf=nf, n_cast=n_cast, relayout=relayout_w_in is not None),
        grid=(nt, nf),
        in_specs=in_specs,
        out_specs=out_specs,
        out_shape=out_shape,
        scratch_shapes=[pltpu.VMEM((tm, d), BF16)],
        compiler_params=_cparams(semantics, vmem),
        name="ffn",
    )(*operands)


def _proj_body(h_ref, w_ref, o_ref):
    o_ref[...] = _dot(h_ref[...], w_ref[...])


def _in_proj(h, w, *, tm=1024, tn=3 * PROJ_TILE):
    t, d = h.shape
    n = w.shape[1]
    tm = min(tm, t)
    vmem = 2 * tm * d * 2 + 2 * d * tn * 2 + 3 * tm * tn * 4
    return pl.pallas_call(
        _proj_body,
        grid=(t // tm, n // tn),
        in_specs=[pl.BlockSpec((tm, d), lambda i, j: (i, 0)), pl.BlockSpec((d, tn), lambda i, j: (0, j))],
        out_specs=pl.BlockSpec((tm, tn), lambda i, j: (i, j)),
        out_shape=jax.ShapeDtypeStruct((t, n), F32),
        compiler_params=_cparams(("parallel", "parallel"), vmem),
        name="in_proj",
    )(h, w)


def _rope_body(pos_ref, inv_ref, o_ref):
    ang = pos_ref[...] * inv_ref[...]
    lane = lax.broadcasted_iota(jnp.int32, ang.shape, 1)
    o_ref[...] = jnp.where(lane < 2 * (MLA_ROPE // 2), jnp.cos(ang), jnp.sin(ang))


def _rope_table(positions):
    b, s = positions.shape
    t = b * s
    inv_freq = jnp.power(ROPE_THETA, -jnp.arange(0, MLA_ROPE, 2, dtype=F32) / MLA_ROPE)
    inv = jnp.tile(inv_freq, LANES // (MLA_ROPE // 2)).reshape(1, LANES)
    pos = jnp.broadcast_to(positions.astype(F32).reshape(t, 1), (t, LANES))
    tm = min(1024, t)
    return pl.pallas_call(
        _rope_body,
        grid=(t // tm,),
        in_specs=[pl.BlockSpec((tm, LANES), lambda i: (i, 0)), pl.BlockSpec((1, LANES), lambda i: (0, 0))],
        out_specs=pl.BlockSpec((tm, LANES), lambda i: (i, 0)),
        out_shape=jax.ShapeDtypeStruct((t, LANES), F32),
        compiler_params=_cparams(("parallel",), 0),
        name="rope_table",
    )(pos, inv)


DN_HALO = SUBLANES


def _dn_body(alog_ref, dtb_ref, q_ref, k_ref, v_ref, z_ref, ab_ref, wq_ref, wk_ref, wv_ref, ng_ref, o_ref,
             state_ref, halo_ref):
    sblk = q_ref.shape[1]
    hd = DN_HEAD_DIM

    @pl.when(pl.program_id(1) == 0)
    def _():
        state_ref[...] = jnp.zeros_like(state_ref)
        halo_ref[...] = jnp.zeros_like(halo_ref)

    lane = lax.broadcasted_iota(jnp.int32, (1, LANES), 1)
    row = lax.broadcasted_iota(jnp.int32, (CHUNK, CHUNK), 0)
    col = lax.broadcasted_iota(jnp.int32, (CHUNK, CHUNK), 1)
    tril = (row >= col).astype(F32)
    row2 = lax.broadcasted_iota(jnp.int32, (2 * CHUNK, 2 * CHUNK), 0)
    col2i = lax.broadcasted_iota(jnp.int32, (2 * CHUNK, 2 * CHUNK), 1)
    same_head = (row2 // CHUNK) == (col2i // CHUNK)
    incl2 = same_head & (row2 >= col2i)
    strict2 = same_head & (row2 > col2i)
    a_scale = jnp.where(lane < DN_HEADS, -jnp.exp(alog_ref[...]), 0.0)
    dt_bias = dtb_ref[...]
    ng = ng_ref[...]
    seq_refs = (q_ref, k_ref, v_ref)
    w_refs = (wq_ref, wk_ref, wv_ref)

    def conv_silu(which, sl, c, c0):
        ref = seq_refs[which]
        cur = ref[0, pl.ds(c0, CHUNK), sl]
        p0 = pl.multiple_of(jnp.maximum(c0 - DN_HALO, 0), DN_HALO)
        prev = jnp.where(c > 0, ref[0, pl.ds(p0, DN_HALO), sl], halo_ref[which, :, sl])
        xx = jnp.concatenate([prev, cur], axis=0)
        w = w_refs[which][:, sl]
        first = DN_HALO - (DN_CONV - 1)
        y = w[0:1] * xx[first:first + CHUNK]
        for j in range(1, DN_CONV):
            y = y + w[j:j + 1] * xx[first + j:first + j + CHUNK]
        return _silu(y)

    heads = range(DN_HEADS)
    pairs = range(DN_HEADS // 2)
    unroll = 4
    items = [(ci, pr) for ci in range(unroll) for pr in pairs]
    sls = [slice(h * hd, (h + 1) * hd) for h in heads]
    top, bot = slice(0, CHUNK), slice(CHUNK, 2 * CHUNK)
    half = lambda h: top if h % 2 == 0 else bot

    def col2(a, off, pr):
        return jnp.concatenate([a[:, off + 2 * pr:off + 2 * pr + 1], a[:, off + 2 * pr + 1:off + 2 * pr + 2]], axis=0)

    def gate_terms(c0):
        ab = ab_ref[0, pl.ds(c0, CHUNK), :]
        xa = ab + dt_bias
        g_all = a_scale * (jnp.maximum(xa, 0.0) + jnp.log1p(jnp.exp(-jnp.abs(xa))))
        beta_all = jax.nn.sigmoid(ab)
        gc_all = jnp.dot(tril, g_all, precision=lax.Precision.HIGHEST, preferred_element_type=F32)
        g_last = gc_all[CHUNK - 1:CHUNK, :]
        gc_t = jnp.concatenate([gc_all, gc_all], axis=0).T
        return dict(beta=beta_all, gc=gc_all, gc_t=gc_t, eg=jnp.exp(gc_all), ed=jnp.exp(g_last - gc_all),
                    egl=jnp.exp(g_last))

    def chunk_group_step(it, carry):
        cidx = [it * unroll + ci for ci in range(unroll)]
        c0s = [pl.multiple_of(c * CHUNK, CHUNK) for c in cidx]
        gt = [gate_terms(c0) for c0 in c0s]

        def stacked(fn, ci, pr):
            return jnp.concatenate([fn(ci, 2 * pr), fn(ci, 2 * pr + 1)], axis=0)

        z2 = {(ci, pr): stacked(lambda ci, h: z_ref[0, pl.ds(c0s[ci], CHUNK), sls[h]], ci, pr) for ci, pr in items}
        q2 = {(ci, pr): stacked(lambda ci, h: conv_silu(0, sls[h], cidx[ci], c0s[ci]), ci, pr) for ci, pr in items}
        k2 = {(ci, pr): stacked(lambda ci, h: conv_silu(1, sls[h], cidx[ci], c0s[ci]), ci, pr) for ci, pr in items}
        v2 = {(ci, pr): stacked(lambda ci, h: conv_silu(2, sls[h], cidx[ci], c0s[ci]), ci, pr) for ci, pr in items}
        q2 = {i: x * lax.rsqrt(jnp.sum(x * x, axis=-1, keepdims=True) + 1e-6) * (hd ** -0.5) for i, x in q2.items()}
        k2 = {i: x * lax.rsqrt(jnp.sum(x * x, axis=-1, keepdims=True) + 1e-6) for i, x in k2.items()}
        beta2 = {(ci, pr): col2(gt[ci]["beta"], DN_HEADS, pr) for ci, pr in items}
        eg2 = {(ci, pr): col2(gt[ci]["eg"], 0, pr) for ci, pr in items}
        g_row2 = {(ci, pr): jnp.where(lane < CHUNK, gt[ci]["gc_t"][2 * pr:2 * pr + 1], gt[ci]["gc_t"][2 * pr + 1:2 * pr + 2])
                  for ci, pr in items}
        decay2 = {(ci, pr): jnp.exp(jnp.where(incl2, col2(gt[ci]["gc"], 0, pr) - g_row2[ci, pr], -jnp.inf))
                  for ci, pr in items}
        kb2 = {i: k2[i] * beta2[i] for i in items}
        km2 = {i: k2[i].astype(BF16) for i in items}
        kk2 = {i: _dot_nt(jnp.concatenate([kb2[i], q2[i]], axis=0).astype(BF16), km2[i]) for i in items}
        qk2 = {i: (kk2[i][2 * CHUNK:] * decay2[i]).astype(BF16) for i in items}
        p2 = {i: jnp.where(strict2, -(kk2[i][:2 * CHUNK] * decay2[i]), 0.0) for i in items}
        sol2 = {i: jnp.concatenate([kb2[i] * eg2[i], v2[i] * beta2[i]], axis=1) for i in items}
        sol2 = {i: sol2[i] + _dot(p2[i].astype(BF16), sol2[i].astype(BF16)) for i in items}
        for _ in range(5):
            pb = {i: p2[i].astype(BF16) for i in items}
            p2 = {i: _dot(pb[i], pb[i]) for i in items}
            sol2 = {i: sol2[i] + _dot(p2[i].astype(BF16), sol2[i].astype(BF16)) for i in items}
        wq2 = {(ci, h): jnp.concatenate([sol2[ci, h // 2][half(h), :hd], (q2[ci, h // 2] * eg2[ci, h // 2])[half(h)]],
                                        axis=0).astype(BF16) for ci in range(unroll) for h in heads}
        kd2 = {(ci, pr): (k2[ci, pr] * col2(gt[ci]["ed"], 0, pr)).astype(BF16) for ci, pr in items}
        states = [state_ref[h] for h in heads]
        for ci in range(unroll):
            ws = [_dot(wq2[ci, h], states[h].astype(BF16)) for h in heads]
            vn2 = [(sol2[ci, pr][:, hd:] - jnp.concatenate([ws[2 * pr][:CHUNK], ws[2 * pr + 1][:CHUNK]], axis=0)).astype(BF16)
                   for pr in pairs]
            states = [states[h] * gt[ci]["egl"][:, h:h + 1] + _dot_tn(kd2[ci, h // 2][half(h)], vn2[h // 2][half(h)])
                      for h in heads]
            o2 = [jnp.concatenate([ws[2 * pr][CHUNK:], ws[2 * pr + 1][CHUNK:]], axis=0) + _dot(qk2[ci, pr], vn2[pr])
                  for pr in pairs]
            out2 = [(_rms(o2[pr]) * ng * _silu(z2[ci, pr])).astype(o_ref.dtype) for pr in pairs]
            for h in heads:
                o_ref[0, pl.ds(c0s[ci], CHUNK), sls[h]] = out2[h // 2][half(h)]
        for h in heads:
            state_ref[h] = states[h]
        return carry

    lax.fori_loop(0, sblk // (CHUNK * unroll), chunk_group_step, 0)
    for which in range(3):
        halo_ref[which] = seq_refs[which][0, sblk - DN_HALO:sblk, :]


def _deltanet(proj3, conv_w, a_log, dt_bias, norm_g, *, sblk=512):
    b, s, _ = proj3.shape
    sblk = min(sblk, s)
    wd = DN_WIDTH
    pad = lambda vec: jnp.concatenate([vec, jnp.zeros((LANES - DN_HEADS,), F32)]).reshape(1, LANES)
    seq_spec = lambda blk: pl.BlockSpec((1, sblk, wd), lambda i, j: (i, j, blk))
    w_spec = lambda blk: pl.BlockSpec((DN_CONV, wd), lambda i, j: (0, blk))
    vec_spec = pl.BlockSpec((1, LANES), lambda i, j: (0, 0))
    live_chunk_values = 16 * 1024 * 1024
    vmem = 2 * 4 * sblk * wd * 4 + 2 * sblk * wd * 2 + 2 * sblk * LANES * 4 + live_chunk_values
    return pl.pallas_call(
        _dn_body,
        grid=(b, s // sblk),
        in_specs=[vec_spec, vec_spec,
                  seq_spec(OFF_QKV // wd), seq_spec(OFF_QKV // wd + 1), seq_spec(OFF_QKV // wd + 2), seq_spec(OFF_Z // wd),
                  pl.BlockSpec((1, sblk, LANES), lambda i, j: (i, j, OFF_AB // LANES)),
                  w_spec(0), w_spec(1), w_spec(2), vec_spec],
        out_specs=pl.BlockSpec((1, sblk, wd), lambda i, j: (i, j, 0)),
        out_shape=jax.ShapeDtypeStruct((b, s, wd), BF16),
        scratch_shapes=[pltpu.VMEM((DN_HEADS, DN_HEAD_DIM, DN_HEAD_DIM), F32), pltpu.VMEM((3, DN_HALO, wd), F32)],
        compiler_params=_cparams(("parallel", "arbitrary"), vmem),
        name="deltanet",
    )(pad(a_log), pad(dt_bias), proj3, proj3, proj3, proj3, proj3, conv_w, conv_w, conv_w, norm_g.reshape(1, -1))


def _gelu(x):
    return 0.5 * x * (1.0 + lax.erf(x * (0.5 ** 0.5)))


def _sg_body(u_ref, v_ref, lng_ref, lnb_ref, w_ref, bt_ref, o_ref):
    u = _gelu(u_ref[0])
    v = _gelu(v_ref[0])
    mu = jnp.mean(v, axis=-1, keepdims=True)
    vc = v - mu
    var = jnp.mean(vc * vc, axis=-1, keepdims=True)
    vn = (vc * lax.rsqrt(var + 1e-5) * lng_ref[...] + lnb_ref[...]).astype(BF16)
    row = lax.broadcasted_iota(jnp.int32, (SG_BLOCK, SG_BLOCK), 0)
    col = lax.broadcasted_iota(jnp.int32, (SG_BLOCK, SG_BLOCK), 1)
    mask = (col // CHUNK) <= (row // CHUNK)
    for g in range(SG_GROUPS):
        sl = slice(g * SG_GROUP_DIM, (g + 1) * SG_GROUP_DIM)
        wg = jnp.where(mask, w_ref[g], 0.0).astype(BF16)
        for blk in range(u.shape[0] // SG_BLOCK):
            rows = slice(blk * SG_BLOCK, (blk + 1) * SG_BLOCK)
            mixed = _dot(wg, vn[rows, sl]) + bt_ref[:, g:g + 1]
            o_ref[0, rows, sl] = (u[rows, sl] * mixed).astype(o_ref.dtype)


def _gmlp(proj3, ln_g, ln_b, sg_w, sg_b, *, blocks_per_step=4):
    b, s, _ = proj3.shape
    ub = OFF_SG // SG_WIDTH
    rows = min(blocks_per_step * SG_BLOCK, s)
    vmem = 2 * 2 * rows * SG_WIDTH * 4 + 2 * SG_GROUPS * SG_BLOCK * SG_BLOCK * 4 + 8 * rows * SG_WIDTH * 4
    return pl.pallas_call(
        _sg_body,
        grid=(b, s // rows),
        in_specs=[
            pl.BlockSpec((1, rows, SG_WIDTH), lambda i, n: (i, n, ub)),
            pl.BlockSpec((1, rows, SG_WIDTH), lambda i, n: (i, n, ub + 1)),
            pl.BlockSpec((1, SG_WIDTH), lambda i, n: (0, 0)),
            pl.BlockSpec((1, SG_WIDTH), lambda i, n: (0, 0)),
            pl.BlockSpec((SG_GROUPS, SG_BLOCK, SG_BLOCK), lambda i, n: (0, 0, 0)),
            pl.BlockSpec((SG_BLOCK, SG_GROUPS), lambda i, n: (0, 0)),
        ],
        out_specs=pl.BlockSpec((1, rows, SG_WIDTH), lambda i, n: (i, n, 0)),
        out_shape=jax.ShapeDtypeStruct((b, s, SG_WIDTH), BF16),
        compiler_params=_cparams(("parallel", "parallel"), vmem),
        name="gmlp",
    )(proj3, proj3, ln_g.reshape(1, SG_WIDTH), ln_b.reshape(1, SG_WIDTH), sg_w, sg_b.T)


def _rotate(block, cs):
    t = block * cs
    return t + pltpu.roll(t, MLA_ROPE, 1)


def _mla_proj_body(cq_ref, ckv_ref, kr_ref, cs_ref, gq_ref, gkv_ref, wq_ref, wk_ref, wvt_ref, q_out, k_out, vt_out):
    cs = cs_ref[0]
    scale = (MLA_NOPE + MLA_ROPE) ** -0.5
    cqn = (_rms(cq_ref[0]) * gq_ref[...]).astype(BF16)
    q = _dot(cqn, wq_ref[...])
    lane = lax.broadcasted_iota(jnp.int32, cs.shape, 1)
    k_rope = jnp.where(lane < MLA_ROPE, _rotate(kr_ref[0], cs), 0.0).astype(k_out.dtype)
    ckvn = (_rms(ckv_ref[0]) * gkv_ref[...]).astype(BF16)
    k = _dot(ckvn, wk_ref[...])
    for h in range(MLA_HEADS):
        base = h * MLA_QW
        q_out[0, :, base:base + MLA_NOPE] = (q[:, base:base + MLA_NOPE] * scale).astype(q_out.dtype)
        q_out[0, :, base + MLA_NOPE:base + MLA_QW] = (_rotate(q[:, base + MLA_NOPE:base + MLA_QW], cs) * scale).astype(q_out.dtype)
        k_out[0, :, base:base + MLA_NOPE] = k[:, h * MLA_NOPE:(h + 1) * MLA_NOPE].astype(k_out.dtype)
        k_out[0, :, base + MLA_NOPE:base + MLA_QW] = k_rope
    vt_out[0] = _dot_nt(wvt_ref[...], ckvn).astype(vt_out.dtype)


def _mla_proj(proj3, cs3, cq_g, ckv_g, w_uq, w_uk, w_uvt, *, tm=512):
    b, s, _ = proj3.shape
    tm = min(tm, s)
    nq = w_uq.shape[1]
    nk = MLA_HEADS * MLA_QW
    nv = w_uvt.shape[0]
    vmem = (2 * tm * (MLA_Q_RANK + MLA_KV_RANK + 2 * LANES) * 4 + 2 * MLA_Q_RANK * (nq + w_uk.shape[1] + nv) * 2
            + 2 * tm * (nq + nk + nv) * 2 + 3 * tm * (nq + nk + nv) * 4)
    const = lambda shape: pl.BlockSpec(shape, lambda bi, i: (0, 0))
    return pl.pallas_call(
        _mla_proj_body,
        grid=(b, s // tm),
        in_specs=[
            pl.BlockSpec((1, tm, MLA_Q_RANK), lambda bi, i: (bi, i, OFF_CQ // MLA_Q_RANK)),
            pl.BlockSpec((1, tm, MLA_KV_RANK), lambda bi, i: (bi, i, OFF_CKV // MLA_KV_RANK)),
            pl.BlockSpec((1, tm, LANES), lambda bi, i: (bi, i, OFF_KR // LANES)),
            pl.BlockSpec((1, tm, LANES), lambda bi, i: (bi, i, 0)),
            const((1, MLA_Q_RANK)), const((1, MLA_KV_RANK)),
            const(w_uq.shape), const(w_uk.shape), const(w_uvt.shape),
        ],
        out_specs=[
            pl.BlockSpec((1, tm, nq), lambda bi, i: (bi, i, 0)),
            pl.BlockSpec((1, tm, nk), lambda bi, i: (bi, i, 0)),
            pl.BlockSpec((1, nv, tm), lambda bi, i: (bi, 0, i)),
        ],
        out_shape=[jax.ShapeDtypeStruct((b, s, nq), BF16), jax.ShapeDtypeStruct((b, s, nk), BF16),
                   jax.ShapeDtypeStruct((b, nv, s), BF16)],
        compiler_params=_cparams(("parallel", "parallel"), vmem),
        name="mla_proj",
    )(proj3, proj3, proj3, cs3, cq_g.reshape(1, -1), ckv_g.reshape(1, -1), w_uq, w_uk, w_uvt)


ATTN_HEADS_PER_STEP = 8


def _attn_body(q_ref, k_ref, vt_ref, o_ref, *, tq, nh):
    i = pl.program_id(2)
    diag_mask = (lax.broadcasted_iota(jnp.int32, (tq, tq), 0) // CHUNK
                 <= lax.broadcasted_iota(jnp.int32, (tq, tq), 1) // CHUNK)

    def kv_step(j, carry, masked):
        k0 = pl.multiple_of(j * tq, tq)
        heads = range(nh)
        st = [_dot_nt(k_ref[0, pl.ds(k0, tq), h * MLA_QW:(h + 1) * MLA_QW], q_ref[0, :, h * MLA_QW:(h + 1) * MLA_QW])
              for h in heads]
        if masked:
            st = [jnp.where(diag_mask, x, -1e30) for x in st]
        m_new = [jnp.maximum(carry[h][0], jnp.max(st[h], axis=0, keepdims=True)) for h in heads]
        alpha = [jnp.exp(carry[h][0] - m_new[h]) for h in heads]
        pt = [jnp.exp(st[h] - m_new[h]) for h in heads]
        l = [alpha[h] * carry[h][1] + jnp.sum(pt[h], axis=0, keepdims=True) for h in heads]
        pv = [_dot(vt_ref[0, h * MLA_V:(h + 1) * MLA_V, pl.ds(k0, tq)], pt[h].astype(BF16)) for h in heads]
        return tuple((m_new[h], l[h], alpha[h] * carry[h][2] + pv[h]) for h in heads)

    init = tuple((jnp.full((1, tq), -jnp.inf, F32), jnp.zeros((1, tq), F32), jnp.zeros((MLA_V, tq), F32))
                 for _ in range(nh))
    carry = lax.fori_loop(0, i, functools.partial(kv_step, masked=False), init)
    carry = kv_step(i, carry, masked=True)
    for h in range(nh):
        _, l, acc = carry[h]
        o_ref[0, :, h * MLA_V:(h + 1) * MLA_V] = (acc / l).T.astype(o_ref.dtype)


def _attention(q3, k3, vt3, *, tq=256, nh=ATTN_HEADS_PER_STEP):
    b, s, _ = q3.shape
    tq = min(tq, s)
    vmem = (2 * tq * nh * MLA_QW * 2 + 2 * s * nh * MLA_QW * 2 + 2 * s * nh * MLA_V * 2 + 2 * tq * nh * MLA_V * 2
            + nh * 8 * tq * tq * 4)
    return pl.pallas_call(
        functools.partial(_attn_body, tq=tq, nh=nh),
        grid=(b, MLA_HEADS // nh, s // tq),
        in_specs=[
            pl.BlockSpec((1, tq, nh * MLA_QW), lambda bi, hg, i: (bi, i, hg)),
            pl.BlockSpec((1, s, nh * MLA_QW), lambda bi, hg, i: (bi, 0, hg)),
            pl.BlockSpec((1, nh * MLA_V, s), lambda bi, hg, i: (bi, hg, 0)),
        ],
        out_specs=pl.BlockSpec((1, tq, nh * MLA_V), lambda bi, hg, i: (bi, i, hg)),
        out_shape=jax.ShapeDtypeStruct((b, s, MLA_HEADS * MLA_V), BF16),
        compiler_params=_cparams(("parallel", "parallel", "arbitrary"), vmem),
        name="mla_attention",
    )(q3, k3, vt3)


def _merge_body(x_ref, oa_ref, ob_ref, oc_ref, ga_ref, gb_ref, gc_ref, wa_ref, wb_ref, wc_ref, wo_ref, g_ref,
                o_ref, *, nc):
    c = pl.program_id(1)

    @pl.when(c == 0)
    def _():
        o_ref[...] = jnp.zeros_like(o_ref)

    merged = jax.nn.sigmoid(ga_ref[...]) * _dot(oa_ref[...], wa_ref[...])
    merged = merged + jax.nn.sigmoid(gb_ref[...]) * _dot(ob_ref[...], wb_ref[...])
    merged = merged + jax.nn.sigmoid(gc_ref[...]) * _dot(oc_ref[...], wc_ref[...])
    o_ref[...] += _dot(merged.astype(BF16), wo_ref[...])

    @pl.when(c == nc - 1)
    def _():
        o_ref[...] = x_ref[...] + _rms(o_ref[...]) * g_ref[...]


def _merge(x, oa, ob, oc, proj, w_branch, w_out, l, g, *, tm=512, tc=512):
    t, d = x.shape
    tm = min(tm, t)
    nc = d // tc
    bw = oa.shape[1]
    vmem = (2 * tm * d * 4 + 2 * tm * d * 4 + 2 * 3 * tm * bw * 2 + 2 * 3 * tm * tc * 4 + 2 * 3 * bw * tc * 2
            + 2 * tc * d * 2 + 6 * tm * tc * 4 + tm * d * 4)
    o_spec = pl.BlockSpec((tm, bw), lambda i, c: (i, 0))
    gate_blk = OFF_GATE // tc
    gate_spec = lambda n: pl.BlockSpec((tm, tc), lambda i, c: (i, gate_blk + n * nc + c))
    wb_spec = lambda n: pl.BlockSpec((None, bw, tc), lambda i, c: (N_BRANCH * l + n, 0, c))
    return pl.pallas_call(
        functools.partial(_merge_body, nc=nc),
        grid=(t // tm, nc),
        in_specs=[pl.BlockSpec((tm, d), lambda i, c: (i, 0)), o_spec, o_spec, o_spec,
                  gate_spec(0), gate_spec(1), gate_spec(2), wb_spec(0), wb_spec(1), wb_spec(2),
                  pl.BlockSpec((None, tc, d), lambda i, c: (l, c, 0)),
                  pl.BlockSpec((1, d), lambda i, c: (0, 0))],
        out_specs=pl.BlockSpec((tm, d), lambda i, c: (i, 0)),
        out_shape=jax.ShapeDtypeStruct((t, d), F32),
        compiler_params=_cparams(("parallel", "arbitrary"), vmem),
        name="merge",
    )(x, oa, ob, oc, proj, proj, proj, w_branch, w_branch, w_branch, w_out, g.reshape(1, d))


def kernel(x, positions, norm_g, ffn_w_gate, ffn_w_up, ffn_w_down, w_in, dn_conv_w, dn_a_log, dn_dt_bias, dn_norm_g, sg_ln_g, sg_ln_b, sg_w, sg_b, mla_cq_norm_g, mla_ckv_norm_g, mla_w_uq, mla_w_ukv, w_branch, w_out):
    b, s, d = x.shape
    t = b * s
    xf = x.reshape(t, d)
    cs = _rope_table(positions)
    wb_all, wo_all = _cast_bf16(w_branch), _cast_bf16(w_out)
    ffn_f32 = tuple(w.reshape((2 * DEPTH,) + w.shape[2:]) for w in (ffn_w_gate, ffn_w_up, ffn_w_down))
    gate_up = _cast_pair_bf16(ffn_f32[0], ffn_f32[1], 0)
    ffn_bf16, wi = [gate_up, gate_up, _cast_bf16(ffn_f32[2], only=0)], (0, 1, 0)
    for l in range(DEPTH):
        ng = norm_g[l]
        xf, *nxt, w_in_bf16, h = _ffn(xf, ng[0], *ffn_bf16, wi, ng[1], cast_next=ffn_f32 + (2 * l + 1,),
                                      relayout_w_in=(w_in, l, ng[2]))
        ffn_bf16, wi = [w[None] for w in nxt], (0, 0, 0)
        proj = _in_proj(h, w_in_bf16)
        proj3 = proj.reshape(b, s, PROJ_COLS)
        o_a = _deltanet(proj3, dn_conv_w[l], dn_a_log[l], dn_dt_bias[l], dn_norm_g[l])
        o_b = _gmlp(proj3, sg_ln_g[l], sg_ln_b[l], sg_w[l], sg_b[l])
        w_uk, w_uvt = _split_w_ukv(mla_w_ukv[l])
        q, k, vt = _mla_proj(proj3, cs.reshape(b, s, LANES), mla_cq_norm_g[l], mla_ckv_norm_g[l],
                             _relayout_w_uq(mla_w_uq[l]), w_uk, w_uvt)
        o_c = _attention(q, k, vt)
        xf = _merge(xf, o_a.reshape(t, -1), o_b.reshape(t, -1), o_c.reshape(t, -1), proj, wb_all, wo_all, l, ng[3])
        xf, *nxt = _ffn(xf, ng[4], *ffn_bf16, wi, ng[5],
                        cast_next=ffn_f32 + (2 * l + 2,) if l + 1 < DEPTH else None)
        ffn_bf16 = [w[None] for w in nxt]
    return xf.reshape(b, s, d)
```

```python
import functools

import jax
import jax.numpy as jnp
from jax import lax
from jax.experimental import pallas as pl
from jax.experimental.pallas import tpu as pltpu

F32 = jnp.float32
BF16 = jnp.bfloat16

D_MODEL = 2048
DEPTH = 2
CHUNK = 64
NORM_EPS = 1e-6
DN_HEADS = 8
DN_HEAD_DIM = 128
DN_WIDTH = DN_HEADS * DN_HEAD_DIM
DN_CONV = 4
SG_GROUPS = 8
SG_GROUP_DIM = 128
SG_WIDTH = SG_GROUPS * SG_GROUP_DIM
SG_BLOCK = 128
MLA_HEADS = 8
MLA_Q_RANK = 512
MLA_KV_RANK = 512
MLA_NOPE = 128
MLA_ROPE = 64
MLA_V = 128
ROPE_THETA = 10000.0
N_BRANCH = 3
BRANCH_WIDTH = 1024
D_FF = 5632

LANES = 128
SUBLANES = 8
V7X_VMEM_BYTES = 64 * 1024 * 1024
VMEM_CAP = V7X_VMEM_BYTES - 6 * 1024 * 1024

OFF_QKV = 0
OFF_Z = OFF_QKV + 3 * DN_WIDTH
OFF_SG = OFF_Z + DN_WIDTH
OFF_GATE = OFF_SG + 2 * SG_WIDTH
OFF_CQ = OFF_GATE + N_BRANCH * D_MODEL
OFF_CKV = OFF_CQ + MLA_Q_RANK
OFF_AB = OFF_CKV + MLA_KV_RANK
OFF_KR = OFF_AB + LANES
PROJ_TILE = 512
PROJ_COLS = -(-(OFF_KR + LANES) // PROJ_TILE) * PROJ_TILE
PROJ_TAIL = PROJ_COLS - OFF_AB
MLA_QW = 2 * LANES

SRC_AB = 4 * DN_WIDTH
SRC_SG = SRC_AB + 2 * DN_HEADS
SRC_CQ = SRC_SG + 2 * SG_WIDTH
SRC_CKV = SRC_CQ + MLA_Q_RANK
SRC_KR = SRC_CKV + MLA_KV_RANK
SRC_GATE = SRC_KR + MLA_ROPE
IN_COLS = SRC_GATE + N_BRANCH * D_MODEL


def _cparams(semantics, vmem_bytes):
    return pltpu.CompilerParams(dimension_semantics=semantics,
                                vmem_limit_bytes=int(min(max(vmem_bytes, 16 * 1024 * 1024), VMEM_CAP)))


def _dot(a, b):
    return jnp.dot(a, b, preferred_element_type=F32)


def _dot_nt(a, b):
    return lax.dot_general(a, b, (((1,), (1,)), ((), ())), preferred_element_type=F32)


def _dot_tn(a, b):
    return lax.dot_general(a, b, (((0,), (0,)), ((), ())), preferred_element_type=F32)


def _rms(x, eps=NORM_EPS):
    return x * lax.rsqrt(jnp.mean(x * x, axis=-1, keepdims=True) + eps)


def _silu(x):
    return x * jax.nn.sigmoid(x)


def _cast_body(w_ref, o_ref):
    o_ref[...] = w_ref[...].astype(o_ref.dtype)


def _cast_bf16(w, only=None, *, block_bytes=4 * 1024 * 1024):
    r, c = w.shape[-2:]
    w = w.reshape(-1, r, c)
    n, first = (w.shape[0], 0) if only is None else (1, only)
    rb = r
    while rb * c * 4 > block_bytes and rb % (4 * SUBLANES) == 0:
        rb //= 2
    return pl.pallas_call(
        _cast_body,
        grid=(n, r // rb),
        in_specs=[pl.BlockSpec((1, rb, c), lambda m, i: (first + m, i, 0))],
        out_specs=pl.BlockSpec((1, rb, c), lambda m, i: (m, i, 0)),
        out_shape=jax.ShapeDtypeStruct((n, r, c), BF16),
        compiler_params=_cparams(("parallel", "parallel"), 3 * 2 * rb * c * 4),
        name="cast_bf16",
    )(w)


_WIDE_SEGMENTS = ((OFF_QKV, 0, SRC_AB), (OFF_SG, SRC_SG, 2 * SG_WIDTH), (OFF_GATE, SRC_GATE, N_BRANCH * D_MODEL),
                  (OFF_CQ, SRC_CQ, MLA_Q_RANK), (OFF_CKV, SRC_CKV, MLA_KV_RANK))


def _cast_pair_body(a_ref, b_ref, o_ref):
    o_ref[0] = a_ref[0].astype(o_ref.dtype)
    o_ref[1] = b_ref[0].astype(o_ref.dtype)


def _cast_pair_bf16(a, b, k, *, block_bytes=2 * 1024 * 1024):
    n, r, c = a.shape
    rb = r
    while rb * c * 4 > block_bytes and rb % (4 * SUBLANES) == 0:
        rb //= 2
    src = pl.BlockSpec((1, rb, c), lambda i: (k, i, 0))
    return pl.pallas_call(
        _cast_pair_body,
        grid=(r // rb,),
        in_specs=[src, src],
        out_specs=pl.BlockSpec((2, rb, c), lambda i: (0, i, 0)),
        out_shape=jax.ShapeDtypeStruct((2, r, c), BF16),
        compiler_params=_cparams(("parallel",), 3 * 2 * 2 * rb * c * 4),
        name="cast_pair_bf16",
    )(a, b)


RELAYOUT_BLOCK = PROJ_TAIL
RELAYOUT_WIDE_BLOCKS = OFF_AB // RELAYOUT_BLOCK
RELAYOUT_BLOCKS = PROJ_COLS // RELAYOUT_BLOCK


def _relayout_step(step, wt_ref, tail_ref, o_ref):
    @pl.when(step < RELAYOUT_WIDE_BLOCKS)
    def _():
        o_ref[...] = wt_ref[0].T.astype(BF16)

    @pl.when(step == RELAYOUT_WIDE_BLOCKS)
    def _():
        o_ref[...] = tail_ref[...].astype(BF16)


def _relayout_src_start(blk):
    dst = blk * RELAYOUT_BLOCK
    start = 0
    for d0, s0, width in _WIDE_SEGMENTS:
        start = jnp.where((dst >= d0) & (dst < d0 + width), s0 + dst - d0, start)
    return pl.multiple_of(start, 2 * SUBLANES)


def _swap_halves(w):
    half = w.shape[-1] // 2
    return jnp.concatenate([-w[..., half:], w[..., :half]], axis=-1)


def _w_in_tail(w_in, l):
    d = w_in.shape[1]
    kr = w_in[l, :, SRC_KR:SRC_GATE]
    return jnp.concatenate([w_in[l, :, SRC_AB:SRC_SG], jnp.zeros((d, LANES - 2 * DN_HEADS), F32), kr, _swap_halves(kr),
                            jnp.zeros((d, PROJ_TAIL - 2 * LANES), F32)], axis=1)


def _relayout_w_uq(w):
    r = w.shape[0]
    w = w.reshape(r, MLA_HEADS, MLA_NOPE + MLA_ROPE)
    rope = w[..., MLA_NOPE:]
    return jnp.concatenate([w[..., :MLA_NOPE], rope, _swap_halves(rope)], axis=-1).reshape(r, MLA_HEADS * MLA_QW).astype(BF16)


def _split_w_ukv(w):
    r = w.shape[0]
    w = w.reshape(r, MLA_HEADS, MLA_NOPE + MLA_V)
    w_uk = w[..., :MLA_NOPE].reshape(r, MLA_HEADS * MLA_NOPE).astype(BF16)
    w_uvt = w[..., MLA_NOPE:].reshape(r, MLA_HEADS * MLA_V).T.astype(BF16)
    return w_uk, w_uvt


def _ffn_body(x_ref, gpre_ref, wg_ref, wu_ref, wd_ref, gpost_ref, *rest, nf, n_cast, relayout, normed_in):
    n_in = n_cast + (3 if relayout else 0)
    cast_src, o_ref, cast_dst = rest[:n_cast], rest[n_in], rest[n_in + 1:n_in + 1 + n_cast]
    f = pl.program_id(1)

    @pl.when(f == 0)
    def _():
        if not normed_in:
            rest[-1][...] = (_rms(x_ref[...]) * gpre_ref[...]).astype(BF16)
        o_ref[...] = jnp.zeros_like(o_ref)

    xn = gpre_ref[...] if normed_in else rest[-1][...]
    h = _dot(xn, wg_ref[...])
    u = _dot(xn, wu_ref[...])
    a = (_silu(h) * u).astype(BF16)
    o_ref[...] += _dot(a, wd_ref[...])
    for src, dst in zip(cast_src, cast_dst):
        dst[...] = src[...].astype(dst.dtype)
    if relayout:
        _relayout_step(pl.program_id(0) * nf + f, rest[n_cast], rest[n_cast + 1], rest[n_in + 1 + n_cast])

    @pl.when(f == nf - 1)
    def _():
        y = x_ref[...] + 0.5 * (_rms(o_ref[...]) * gpost_ref[...])
        o_ref[...] = y
        if relayout:
            rest[n_in + 2 + n_cast][...] = (_rms(y) * rest[n_cast + 2][...]).astype(BF16)


def _ffn(x, g_pre, w_gate, w_up, w_down, wi, g_post, cast_next=None, relayout_w_in=None, *, tm=512, tf=512):
    t, d = x.shape
    tm = min(tm, t)
    dff = w_gate.shape[-1]
    nf = dff // tf
    nt = t // tm
    wi_gate, wi_up, wi_down = wi
    normed_in = g_pre.ndim == 2
    vmem = (2 * tm * d * 4 + 2 * tm * d * 4 + 2 * tm * d * 2 + 2 * 3 * d * tf * 2 + 5 * tm * tf * 4 + tm * d * 4)
    in_specs = [
        pl.BlockSpec((tm, d), lambda i, f: (i, 0)),
        pl.BlockSpec((tm, d), lambda i, f: (i, 0)) if normed_in else pl.BlockSpec((1, d), lambda i, f: (0, 0)),
        pl.BlockSpec((None, d, tf), lambda i, f: (wi_gate, 0, f)),
        pl.BlockSpec((None, d, tf), lambda i, f: (wi_up, 0, f)),
        pl.BlockSpec((None, tf, d), lambda i, f: (wi_down, f, 0)),
        pl.BlockSpec((1, d), lambda i, f: (0, 0)),
    ]
    out_specs = [pl.BlockSpec((tm, d), lambda i, f: (i, 0))]
    out_shape = [jax.ShapeDtypeStruct((t, d), F32)]
    operands = [x, g_pre if normed_in else g_pre.reshape(1, d), w_gate, w_up, w_down, g_post.reshape(1, d)]
    n_cast = 0
    if cast_next is not None:
        *srcs, ci = cast_next
        n_cast = len(srcs)
        dr = d // nt
        in_specs += [pl.BlockSpec((None, dr, tf), lambda i, f: (ci, i, f)),
                     pl.BlockSpec((None, dr, tf), lambda i, f: (ci, i, f)),
                     pl.BlockSpec((None, tf, dr), lambda i, f: (ci, f, i))]
        out_specs += [pl.BlockSpec((dr, tf), lambda i, f: (i, f)), pl.BlockSpec((dr, tf), lambda i, f: (i, f)),
                      pl.BlockSpec((tf, dr), lambda i, f: (f, i))]
        out_shape += [jax.ShapeDtypeStruct(s.shape[-2:], BF16) for s in srcs]
        operands += srcs
        vmem += 2 * 3 * dr * tf * (4 + 2)
    if relayout_w_in is not None:
        w_in, l, g_proj = relayout_w_in
        assert nt * nf >= RELAYOUT_BLOCKS
        tb = RELAYOUT_BLOCK
        blk = lambda i, f: jnp.minimum(i * nf + f, RELAYOUT_BLOCKS - 1)
        in_specs += [pl.BlockSpec((pl.Element(1), pl.Element(tb), pl.Element(d)),
                                  lambda i, f: (l, _relayout_src_start(blk(i, f)), 0)),
                     pl.BlockSpec((d, tb), lambda i, f: (0, 0)),
                     pl.BlockSpec((1, d), lambda i, f: (0, 0))]
        out_specs += [pl.BlockSpec((d, tb), lambda i, f: (0, blk(i, f))), pl.BlockSpec((tm, d), lambda i, f: (i, 0))]
        out_shape += [jax.ShapeDtypeStruct((d, PROJ_COLS), BF16), jax.ShapeDtypeStruct((t, d), BF16)]
        operands += [jnp.swapaxes(w_in, 1, 2), _w_in_tail(w_in, l), g_proj.reshape(1, d)]
        vmem += 2 * 2 * tb * d * 4 + 2 * tb * d * 2 + tb * d * 4 + 2 * tm * d * 2
    semantics = ("parallel", "arbitrary") if relayout_w_in is None else ("arbitrary", "arbitrary")
    return pl.pallas_call(
        functools.partial(_ffn_body, nf=nf, n_cast=n_cast, relayout=relayout_w_in is not None, normed_in=normed_in),
        grid=(nt, nf),
        in_specs=in_specs,
        out_specs=out_specs,
        out_shape=out_shape,
        scratch_shapes=[] if normed_in else [pltpu.VMEM((tm, d), BF16)],
        compiler_params=_cparams(semantics, vmem),
        name="ffn",
    )(*operands)


def _proj_body(h_ref, w_ref, o_ref):
    o_ref[...] = _dot(h_ref[...], w_ref[...])


def _in_proj(h, w, *, tm=1024, tn=3 * PROJ_TILE):
    t, d = h.shape
    n = w.shape[1]
    tm = min(tm, t)
    vmem = 2 * tm * d * 2 + 2 * d * tn * 2 + 3 * tm * tn * 4
    return pl.pallas_call(
        _proj_body,
        grid=(t // tm, n // tn),
        in_specs=[pl.BlockSpec((tm, d), lambda i, j: (i, 0)), pl.BlockSpec((d, tn), lambda i, j: (0, j))],
        out_specs=pl.BlockSpec((tm, tn), lambda i, j: (i, j)),
        out_shape=jax.ShapeDtypeStruct((t, n), F32),
        compiler_params=_cparams(("parallel", "parallel"), vmem),
        name="in_proj",
    )(h, w)


def _rope_body(pos_ref, inv_ref, o_ref):
    ang = pos_ref[...] * inv_ref[...]
    lane = lax.broadcasted_iota(jnp.int32, ang.shape, 1)
    o_ref[...] = jnp.where(lane < 2 * (MLA_ROPE // 2), jnp.cos(ang), jnp.sin(ang))


def _rope_table(positions):
    b, s = positions.shape
    t = b * s
    inv_freq = jnp.power(ROPE_THETA, -jnp.arange(0, MLA_ROPE, 2, dtype=F32) / MLA_ROPE)
    inv = jnp.tile(inv_freq, LANES // (MLA_ROPE // 2)).reshape(1, LANES)
    pos = jnp.broadcast_to(positions.astype(F32).reshape(t, 1), (t, LANES))
    tm = min(1024, t)
    return pl.pallas_call(
        _rope_body,
        grid=(t // tm,),
        in_specs=[pl.BlockSpec((tm, LANES), lambda i: (i, 0)), pl.BlockSpec((1, LANES), lambda i: (0, 0))],
        out_specs=pl.BlockSpec((tm, LANES), lambda i: (i, 0)),
        out_shape=jax.ShapeDtypeStruct((t, LANES), F32),
        compiler_params=_cparams(("parallel",), 0),
        name="rope_table",
    )(pos, inv)


DN_HALO = SUBLANES


def _dn_body(alog_ref, dtb_ref, q_ref, k_ref, v_ref, z_ref, ab_ref, wq_ref, wk_ref, wv_ref, ng_ref, o_ref,
             state_ref, halo_ref):
    sblk = q_ref.shape[1]
    hd = DN_HEAD_DIM

    @pl.when(pl.program_id(1) == 0)
    def _():
        state_ref[...] = jnp.zeros_like(state_ref)
        halo_ref[...] = jnp.zeros_like(halo_ref)

    lane = lax.broadcasted_iota(jnp.int32, (1, LANES), 1)
    row = lax.broadcasted_iota(jnp.int32, (CHUNK, CHUNK), 0)
    col = lax.broadcasted_iota(jnp.int32, (CHUNK, CHUNK), 1)
    tril = (row >= col).astype(F32)
    row2 = lax.broadcasted_iota(jnp.int32, (2 * CHUNK, 2 * CHUNK), 0)
    col2i = lax.broadcasted_iota(jnp.int32, (2 * CHUNK, 2 * CHUNK), 1)
    same_head = (row2 // CHUNK) == (col2i // CHUNK)
    incl2 = same_head & (row2 >= col2i)
    strict2 = same_head & (row2 > col2i)
    a_scale = jnp.where(lane < DN_HEADS, -jnp.exp(alog_ref[...]), 0.0)
    dt_bias = dtb_ref[...]
    ng = ng_ref[...]
    seq_refs = (q_ref, k_ref, v_ref)
    w_refs = (wq_ref, wk_ref, wv_ref)

    def conv_silu(which, sl, c, c0):
        ref = seq_refs[which]
        cur = ref[0, pl.ds(c0, CHUNK), sl]
        p0 = pl.multiple_of(jnp.maximum(c0 - DN_HALO, 0), DN_HALO)
        prev = jnp.where(c > 0, ref[0, pl.ds(p0, DN_HALO), sl], halo_ref[which, :, sl])
        xx = jnp.concatenate([prev, cur], axis=0)
        w = w_refs[which][:, sl]
        first = DN_HALO - (DN_CONV - 1)
        y = w[0:1] * xx[first:first + CHUNK]
        for j in range(1, DN_CONV):
            y = y + w[j:j + 1] * xx[first + j:first + j + CHUNK]
        return _silu(y)

    heads = range(DN_HEADS)
    pairs = range(DN_HEADS // 2)
    unroll = 4
    items = [(ci, pr) for ci in range(unroll) for pr in pairs]
    sls = [slice(h * hd, (h + 1) * hd) for h in heads]
    top, bot = slice(0, CHUNK), slice(CHUNK, 2 * CHUNK)
    half = lambda h: top if h % 2 == 0 else bot

    def col2(a, off, pr):
        return jnp.concatenate([a[:, off + 2 * pr:off + 2 * pr + 1], a[:, off + 2 * pr + 1:off + 2 * pr + 2]], axis=0)

    def gate_terms(c0):
        ab = ab_ref[0, pl.ds(c0, CHUNK), :]
        xa = ab + dt_bias
        g_all = a_scale * (jnp.maximum(xa, 0.0) + jnp.log1p(jnp.exp(-jnp.abs(xa))))
        beta_all = jax.nn.sigmoid(ab)
        gc_all = jnp.dot(tril, g_all, precision=lax.Precision.HIGHEST, preferred_element_type=F32)
        g_last = gc_all[CHUNK - 1:CHUNK, :]
        gc_t = jnp.concatenate([gc_all, gc_all], axis=0).T
        return dict(beta=beta_all, gc=gc_all, gc_t=gc_t, eg=jnp.exp(gc_all), ed=jnp.exp(g_last - gc_all),
                    egl=jnp.exp(g_last))

    def chunk_group_step(it, carry):
        cidx = [it * unroll + ci for ci in range(unroll)]
        c0s = [pl.multiple_of(c * CHUNK, CHUNK) for c in cidx]
        gt = [gate_terms(c0) for c0 in c0s]

        def stacked(fn, ci, pr):
            return jnp.concatenate([fn(ci, 2 * pr), fn(ci, 2 * pr + 1)], axis=0)

        z2 = {(ci, pr): stacked(lambda ci, h: z_ref[0, pl.ds(c0s[ci], CHUNK), sls[h]], ci, pr) for ci, pr in items}
        q2 = {(ci, pr): stacked(lambda ci, h: conv_silu(0, sls[h], cidx[ci], c0s[ci]), ci, pr) for ci, pr in items}
        k2 = {(ci, pr): stacked(lambda ci, h: conv_silu(1, sls[h], cidx[ci], c0s[ci]), ci, pr) for ci, pr in items}
        v2 = {(ci, pr): stacked(lambda ci, h: conv_silu(2, sls[h], cidx[ci], c0s[ci]), ci, pr) for ci, pr in items}
        q2 = {i: x * lax.rsqrt(jnp.sum(x * x, axis=-1, keepdims=True) + 1e-6) * (hd ** -0.5) for i, x in q2.items()}
        k2 = {i: x * lax.rsqrt(jnp.sum(x * x, axis=-1, keepdims=True) + 1e-6) for i, x in k2.items()}
        beta2 = {(ci, pr): col2(gt[ci]["beta"], DN_HEADS, pr) for ci, pr in items}
        eg2 = {(ci, pr): col2(gt[ci]["eg"], 0, pr) for ci, pr in items}
        g_row2 = {(ci, pr): jnp.where(lane < CHUNK, gt[ci]["gc_t"][2 * pr:2 * pr + 1], gt[ci]["gc_t"][2 * pr + 1:2 * pr + 2])
                  for ci, pr in items}
        decay2 = {(ci, pr): jnp.exp(jnp.where(incl2, col2(gt[ci]["gc"], 0, pr) - g_row2[ci, pr], -jnp.inf))
                  for ci, pr in items}
        kb2 = {i: k2[i] * beta2[i] for i in items}
        km2 = {i: k2[i].astype(BF16) for i in items}
        kk2 = {i: _dot_nt(jnp.concatenate([kb2[i], q2[i]], axis=0).astype(BF16), km2[i]) for i in items}
        qk2 = {i: (kk2[i][2 * CHUNK:] * decay2[i]).astype(BF16) for i in items}
        p2 = {i: jnp.where(strict2, -(kk2[i][:2 * CHUNK] * decay2[i]), 0.0) for i in items}
        sol2 = {i: jnp.concatenate([kb2[i] * eg2[i], v2[i] * beta2[i]], axis=1) for i in items}
        sol2 = {i: sol2[i] + _dot(p2[i].astype(BF16), sol2[i].astype(BF16)) for i in items}
        for _ in range(5):
            pb = {i: p2[i].astype(BF16) for i in items}
            p2 = {i: _dot(pb[i], pb[i]) for i in items}
            sol2 = {i: sol2[i] + _dot(p2[i].astype(BF16), sol2[i].astype(BF16)) for i in items}
        wq2 = {(ci, h): jnp.concatenate([sol2[ci, h // 2][half(h), :hd], (q2[ci, h // 2] * eg2[ci, h // 2])[half(h)]],
                                        axis=0).astype(BF16) for ci in range(unroll) for h in heads}
        kd2 = {(ci, pr): (k2[ci, pr] * col2(gt[ci]["ed"], 0, pr)).astype(BF16) for ci, pr in items}
        states = [state_ref[h] for h in heads]
        for ci in range(unroll):
            ws = [_dot(wq2[ci, h], states[h].astype(BF16)) for h in heads]
            vn2 = [(sol2[ci, pr][:, hd:] - jnp.concatenate([ws[2 * pr][:CHUNK], ws[2 * pr + 1][:CHUNK]], axis=0)).astype(BF16)
                   for pr in pairs]
            states = [states[h] * gt[ci]["egl"][:, h:h + 1] + _dot_tn(kd2[ci, h // 2][half(h)], vn2[h // 2][half(h)])
                      for h in heads]
            o2 = [jnp.concatenate([ws[2 * pr][CHUNK:], ws[2 * pr + 1][CHUNK:]], axis=0) + _dot(qk2[ci, pr], vn2[pr])
                  for pr in pairs]
            out2 = [(_rms(o2[pr]) * ng * _silu(z2[ci, pr])).astype(o_ref.dtype) for pr in pairs]
            for h in heads:
                o_ref[0, pl.ds(c0s[ci], CHUNK), sls[h]] = out2[h // 2][half(h)]
        for h in heads:
            state_ref[h] = states[h]
        return carry

    lax.fori_loop(0, sblk // (CHUNK * unroll), chunk_group_step, 0)
    for which in range(3):
        halo_ref[which] = seq_refs[which][0, sblk - DN_HALO:sblk, :]


def _deltanet(proj3, conv_w, a_log, dt_bias, norm_g, *, sblk=512):
    b, s, _ = proj3.shape
    sblk = min(sblk, s)
    wd = DN_WIDTH
    pad = lambda vec: jnp.concatenate([vec, jnp.zeros((LANES - DN_HEADS,), F32)]).reshape(1, LANES)
    seq_spec = lambda blk: pl.BlockSpec((1, sblk, wd), lambda i, j: (i, j, blk))
    w_spec = lambda blk: pl.BlockSpec((DN_CONV, wd), lambda i, j: (0, blk))
    vec_spec = pl.BlockSpec((1, LANES), lambda i, j: (0, 0))
    live_chunk_values = 16 * 1024 * 1024
    vmem = 2 * 4 * sblk * wd * 4 + 2 * sblk * wd * 2 + 2 * sblk * LANES * 4 + live_chunk_values
    return pl.pallas_call(
        _dn_body,
        grid=(b, s // sblk),
        in_specs=[vec_spec, vec_spec,
                  seq_spec(OFF_QKV // wd), seq_spec(OFF_QKV // wd + 1), seq_spec(OFF_QKV // wd + 2), seq_spec(OFF_Z // wd),
                  pl.BlockSpec((1, sblk, LANES), lambda i, j: (i, j, OFF_AB // LANES)),
                  w_spec(0), w_spec(1), w_spec(2), vec_spec],
        out_specs=pl.BlockSpec((1, sblk, wd), lambda i, j: (i, j, 0)),
        out_shape=jax.ShapeDtypeStruct((b, s, wd), BF16),
        scratch_shapes=[pltpu.VMEM((DN_HEADS, DN_HEAD_DIM, DN_HEAD_DIM), F32), pltpu.VMEM((3, DN_HALO, wd), F32)],
        compiler_params=_cparams(("parallel", "arbitrary"), vmem),
        name="deltanet",
    )(pad(a_log), pad(dt_bias), proj3, proj3, proj3, proj3, proj3, conv_w, conv_w, conv_w, norm_g.reshape(1, -1))


def _gelu(x):
    return 0.5 * x * (1.0 + lax.erf(x * (0.5 ** 0.5)))


def _sg_body(u_ref, v_ref, lng_ref, lnb_ref, w_ref, bt_ref, o_ref):
    u = _gelu(u_ref[0])
    v = _gelu(v_ref[0])
    mu = jnp.mean(v, axis=-1, keepdims=True)
    vc = v - mu
    var = jnp.mean(vc * vc, axis=-1, keepdims=True)
    vn = (vc * lax.rsqrt(var + 1e-5) * lng_ref[...] + lnb_ref[...]).astype(BF16)
    row = lax.broadcasted_iota(jnp.int32, (SG_BLOCK, SG_BLOCK), 0)
    col = lax.broadcasted_iota(jnp.int32, (SG_BLOCK, SG_BLOCK), 1)
    mask = (col // CHUNK) <= (row // CHUNK)
    for g in range(SG_GROUPS):
        sl = slice(g * SG_GROUP_DIM, (g + 1) * SG_GROUP_DIM)
        wg = jnp.where(mask, w_ref[g], 0.0).astype(BF16)
        for blk in range(u.shape[0] // SG_BLOCK):
            rows = slice(blk * SG_BLOCK, (blk + 1) * SG_BLOCK)
            mixed = _dot(wg, vn[rows, sl]) + bt_ref[:, g:g + 1]
            o_ref[0, rows, sl] = (u[rows, sl] * mixed).astype(o_ref.dtype)


def _gmlp(proj3, ln_g, ln_b, sg_w, sg_b, *, blocks_per_step=4):
    b, s, _ = proj3.shape
    ub = OFF_SG // SG_WIDTH
    rows = min(blocks_per_step * SG_BLOCK, s)
    vmem = 2 * 2 * rows * SG_WIDTH * 4 + 2 * SG_GROUPS * SG_BLOCK * SG_BLOCK * 4 + 8 * rows * SG_WIDTH * 4
    return pl.pallas_call(
        _sg_body,
        grid=(b, s // rows),
        in_specs=[
            pl.BlockSpec((1, rows, SG_WIDTH), lambda i, n: (i, n, ub)),
            pl.BlockSpec((1, rows, SG_WIDTH), lambda i, n: (i, n, ub + 1)),
            pl.BlockSpec((1, SG_WIDTH), lambda i, n: (0, 0)),
            pl.BlockSpec((1, SG_WIDTH), lambda i, n: (0, 0)),
            pl.BlockSpec((SG_GROUPS, SG_BLOCK, SG_BLOCK), lambda i, n: (0, 0, 0)),
            pl.BlockSpec((SG_BLOCK, SG_GROUPS), lambda i, n: (0, 0)),
        ],
        out_specs=pl.BlockSpec((1, rows, SG_WIDTH), lambda i, n: (i, n, 0)),
        out_shape=jax.ShapeDtypeStruct((b, s, SG_WIDTH), BF16),
        compiler_params=_cparams(("parallel", "parallel"), vmem),
        name="gmlp",
    )(proj3, proj3, ln_g.reshape(1, SG_WIDTH), ln_b.reshape(1, SG_WIDTH), sg_w, sg_b.T)


def _rotate(block, cs):
    t = block * cs
    return t + pltpu.roll(t, MLA_ROPE, 1)


def _mla_proj_body(cq_ref, ckv_ref, kr_ref, cs_ref, gq_ref, gkv_ref, wq_ref, wk_ref, wvt_ref, q_out, k_out, vt_out):
    cs = cs_ref[0]
    scale = (MLA_NOPE + MLA_ROPE) ** -0.5
    cqn = (_rms(cq_ref[0]) * gq_ref[...]).astype(BF16)
    q = _dot(cqn, wq_ref[...])
    lane = lax.broadcasted_iota(jnp.int32, cs.shape, 1)
    k_rope = jnp.where(lane < MLA_ROPE, _rotate(kr_ref[0], cs), 0.0).astype(k_out.dtype)
    ckvn = (_rms(ckv_ref[0]) * gkv_ref[...]).astype(BF16)
    k = _dot(ckvn, wk_ref[...])
    for h in range(MLA_HEADS):
        base = h * MLA_QW
        q_out[0, :, base:base + MLA_NOPE] = (q[:, base:base + MLA_NOPE] * scale).astype(q_out.dtype)
        q_out[0, :, base + MLA_NOPE:base + MLA_QW] = (_rotate(q[:, base + MLA_NOPE:base + MLA_QW], cs) * scale).astype(q_out.dtype)
        k_out[0, :, base:base + MLA_NOPE] = k[:, h * MLA_NOPE:(h + 1) * MLA_NOPE].astype(k_out.dtype)
        k_out[0, :, base + MLA_NOPE:base + MLA_QW] = k_rope
    vt_out[0] = _dot_nt(wvt_ref[...], ckvn).astype(vt_out.dtype)


def _mla_proj(proj3, cs3, cq_g, ckv_g, w_uq, w_uk, w_uvt, *, tm=512):
    b, s, _ = proj3.shape
    tm = min(tm, s)
    nq = w_uq.shape[1]
    nk = MLA_HEADS * MLA_QW
    nv = w_uvt.shape[0]
    vmem = (2 * tm * (MLA_Q_RANK + MLA_KV_RANK + 2 * LANES) * 4 + 2 * MLA_Q_RANK * (nq + w_uk.shape[1] + nv) * 2
            + 2 * tm * (nq + nk + nv) * 2 + 3 * tm * (nq + nk + nv) * 4)
    const = lambda shape: pl.BlockSpec(shape, lambda bi, i: (0, 0))
    return pl.pallas_call(
        _mla_proj_body,
        grid=(b, s // tm),
        in_specs=[
            pl.BlockSpec((1, tm, MLA_Q_RANK), lambda bi, i: (bi, i, OFF_CQ // MLA_Q_RANK)),
            pl.BlockSpec((1, tm, MLA_KV_RANK), lambda bi, i: (bi, i, OFF_CKV // MLA_KV_RANK)),
            pl.BlockSpec((1, tm, LANES), lambda bi, i: (bi, i, OFF_KR // LANES)),
            pl.BlockSpec((1, tm, LANES), lambda bi, i: (bi, i, 0)),
            const((1, MLA_Q_RANK)), const((1, MLA_KV_RANK)),
            const(w_uq.shape), const(w_uk.shape), const(w_uvt.shape),
        ],
        out_specs=[
            pl.BlockSpec((1, tm, nq), lambda bi, i: (bi, i, 0)),
            pl.BlockSpec((1, tm, nk), lambda bi, i: (bi, i, 0)),
            pl.BlockSpec((1, nv, tm), lambda bi, i: (bi, 0, i)),
        ],
        out_shape=[jax.ShapeDtypeStruct((b, s, nq), BF16), jax.ShapeDtypeStruct((b, s, nk), BF16),
                   jax.ShapeDtypeStruct((b, nv, s), BF16)],
        compiler_params=_cparams(("parallel", "parallel"), vmem),
        name="mla_proj",
    )(proj3, proj3, proj3, cs3, cq_g.reshape(1, -1), ckv_g.reshape(1, -1), w_uq, w_uk, w_uvt)


ATTN_HEADS_PER_STEP = 8


def _attn_body(q_ref, k_ref, vt_ref, o_ref, *, tq, nh):
    i = pl.program_id(2)
    diag_mask = (lax.broadcasted_iota(jnp.int32, (tq, tq), 0) // CHUNK
                 <= lax.broadcasted_iota(jnp.int32, (tq, tq), 1) // CHUNK)

    def kv_step(j, carry, masked):
        k0 = pl.multiple_of(j * tq, tq)
        heads = range(nh)
        st = [_dot_nt(k_ref[0, pl.ds(k0, tq), h * MLA_QW:(h + 1) * MLA_QW], q_ref[0, :, h * MLA_QW:(h + 1) * MLA_QW])
              for h in heads]
        if masked:
            st = [jnp.where(diag_mask, x, -1e30) for x in st]
        m_new = [jnp.maximum(carry[h][0], jnp.max(st[h], axis=0, keepdims=True)) for h in heads]
        alpha = [jnp.exp(carry[h][0] - m_new[h]) for h in heads]
        pt = [jnp.exp(st[h] - m_new[h]) for h in heads]
        l = [alpha[h] * carry[h][1] + jnp.sum(pt[h], axis=0, keepdims=True) for h in heads]
        pv = [_dot(vt_ref[0, h * MLA_V:(h + 1) * MLA_V, pl.ds(k0, tq)], pt[h].astype(BF16)) for h in heads]
        return tuple((m_new[h], l[h], alpha[h] * carry[h][2] + pv[h]) for h in heads)

    init = tuple((jnp.full((1, tq), -jnp.inf, F32), jnp.zeros((1, tq), F32), jnp.zeros((MLA_V, tq), F32))
                 for _ in range(nh))
    carry = lax.fori_loop(0, i, functools.partial(kv_step, masked=False), init)
    carry = kv_step(i, carry, masked=True)
    for h in range(nh):
        _, l, acc = carry[h]
        o_ref[0, :, h * MLA_V:(h + 1) * MLA_V] = (acc / l).T.astype(o_ref.dtype)


def _attention(q3, k3, vt3, *, tq=256, nh=ATTN_HEADS_PER_STEP):
    b, s, _ = q3.shape
    tq = min(tq, s)
    vmem = (2 * tq * nh * MLA_QW * 2 + 2 * s * nh * MLA_QW * 2 + 2 * s * nh * MLA_V * 2 + 2 * tq * nh * MLA_V * 2
            + nh * 8 * tq * tq * 4)
    return pl.pallas_call(
        functools.partial(_attn_body, tq=tq, nh=nh),
        grid=(b, MLA_HEADS // nh, s // tq),
        in_specs=[
            pl.BlockSpec((1, tq, nh * MLA_QW), lambda bi, hg, i: (bi, i, hg)),
            pl.BlockSpec((1, s, nh * MLA_QW), lambda bi, hg, i: (bi, 0, hg)),
            pl.BlockSpec((1, nh * MLA_V, s), lambda bi, hg, i: (bi, hg, 0)),
        ],
        out_specs=pl.BlockSpec((1, tq, nh * MLA_V), lambda bi, hg, i: (bi, i, hg)),
        out_shape=jax.ShapeDtypeStruct((b, s, MLA_HEADS * MLA_V), BF16),
        compiler_params=_cparams(("parallel", "parallel", "arbitrary"), vmem),
        name="mla_attention",
    )(q3, k3, vt3)


def _merge_body(x_ref, oa_ref, ob_ref, oc_ref, ga_ref, gb_ref, gc_ref, wa_ref, wb_ref, wc_ref, wo_ref, g_ref,
                gn_ref, o_ref, hn_ref, *, nc):
    c = pl.program_id(1)

    @pl.when(c == 0)
    def _():
        o_ref[...] = jnp.zeros_like(o_ref)

    merged = jax.nn.sigmoid(ga_ref[...]) * _dot(oa_ref[...], wa_ref[...])
    merged = merged + jax.nn.sigmoid(gb_ref[...]) * _dot(ob_ref[...], wb_ref[...])
    merged = merged + jax.nn.sigmoid(gc_ref[...]) * _dot(oc_ref[...], wc_ref[...])
    o_ref[...] += _dot(merged.astype(BF16), wo_ref[...])

    @pl.when(c == nc - 1)
    def _():
        y = x_ref[...] + _rms(o_ref[...]) * g_ref[...]
        o_ref[...] = y
        hn_ref[...] = (_rms(y) * gn_ref[...]).astype(hn_ref.dtype)


def _merge(x, oa, ob, oc, proj, w_branch, w_out, l, g, g_next, *, tm=512, tc=512):
    t, d = x.shape
    tm = min(tm, t)
    nc = d // tc
    bw = oa.shape[1]
    vmem = (2 * tm * d * 4 + 2 * tm * d * 4 + 2 * 3 * tm * bw * 2 + 2 * 3 * tm * tc * 4 + 2 * 3 * bw * tc * 2
            + 2 * tc * d * 2 + 6 * tm * tc * 4 + tm * d * 4)
    o_spec = pl.BlockSpec((tm, bw), lambda i, c: (i, 0))
    gate_blk = OFF_GATE // tc
    gate_spec = lambda n: pl.BlockSpec((tm, tc), lambda i, c: (i, gate_blk + n * nc + c))
    wb_spec = lambda n: pl.BlockSpec((None, bw, tc), lambda i, c: (N_BRANCH * l + n, 0, c))
    return pl.pallas_call(
        functools.partial(_merge_body, nc=nc),
        grid=(t // tm, nc),
        in_specs=[pl.BlockSpec((tm, d), lambda i, c: (i, 0)), o_spec, o_spec, o_spec,
                  gate_spec(0), gate_spec(1), gate_spec(2), wb_spec(0), wb_spec(1), wb_spec(2),
                  pl.BlockSpec((None, tc, d), lambda i, c: (l, c, 0)),
                  pl.BlockSpec((1, d), lambda i, c: (0, 0)), pl.BlockSpec((1, d), lambda i, c: (0, 0))],
        out_specs=[pl.BlockSpec((tm, d), lambda i, c: (i, 0)), pl.BlockSpec((tm, d), lambda i, c: (i, 0))],
        out_shape=[jax.ShapeDtypeStruct((t, d), F32), jax.ShapeDtypeStruct((t, d), BF16)],
        compiler_params=_cparams(("parallel", "arbitrary"), vmem + 2 * tm * d * 2),
        name="merge",
    )(x, oa, ob, oc, proj, proj, proj, w_branch, w_branch, w_branch, w_out, g.reshape(1, d), g_next.reshape(1, d))


def kernel(x, positions, norm_g, ffn_w_gate, ffn_w_up, ffn_w_down, w_in, dn_conv_w, dn_a_log, dn_dt_bias, dn_norm_g, sg_ln_g, sg_ln_b, sg_w, sg_b, mla_cq_norm_g, mla_ckv_norm_g, mla_w_uq, mla_w_ukv, w_branch, w_out):
    b, s, d = x.shape
    t = b * s
    xf = x.reshape(t, d)
    cs = _rope_table(positions)
    wb_all, wo_all = _cast_bf16(w_branch), _cast_bf16(w_out)
    ffn_f32 = tuple(w.reshape((2 * DEPTH,) + w.shape[2:]) for w in (ffn_w_gate, ffn_w_up, ffn_w_down))
    gate_up = _cast_pair_bf16(ffn_f32[0], ffn_f32[1], 0)
    ffn_bf16, wi = [gate_up, gate_up, _cast_bf16(ffn_f32[2], only=0)], (0, 1, 0)
    for l in range(DEPTH):
        ng = norm_g[l]
        xf, *nxt, w_in_bf16, h = _ffn(xf, ng[0], *ffn_bf16, wi, ng[1], cast_next=ffn_f32 + (2 * l + 1,),
                                      relayout_w_in=(w_in, l, ng[2]))
        ffn_bf16, wi = [w[None] for w in nxt], (0, 0, 0)
        proj = _in_proj(h, w_in_bf16)
        proj3 = proj.reshape(b, s, PROJ_COLS)
        o_a = _deltanet(proj3, dn_conv_w[l], dn_a_log[l], dn_dt_bias[l], dn_norm_g[l])
        o_b = _gmlp(proj3, sg_ln_g[l], sg_ln_b[l], sg_w[l], sg_b[l])
        w_uk, w_uvt = _split_w_ukv(mla_w_ukv[l])
        q, k, vt = _mla_proj(proj3, cs.reshape(b, s, LANES), mla_cq_norm_g[l], mla_ckv_norm_g[l],
                             _relayout_w_uq(mla_w_uq[l]), w_uk, w_uvt)
        o_c = _attention(q, k, vt)
        xf, xn = _merge(xf, o_a.reshape(t, -1), o_b.reshape(t, -1), o_c.reshape(t, -1), proj, wb_all, wo_all, l,
                        ng[3], ng[4])
        xf, *nxt = _ffn(xf, xn, *ffn_bf16, wi, ng[5],
                        cast_next=ffn_f32 + (2 * l + 2,) if l + 1 < DEPTH else None)
        ffn_bf16 = [w[None] for w in nxt]
    return xf.reshape(b, s, d)
```
